```python
import math
import jax, jax.numpy as jnp
from jax import lax
import numpy as np

D_MODEL = 1024
BATCH = 8
SEQ = 8192
DEPTH = 4
DEC_BATCH = 16
DEC_SEQ = 64
PAST_LEN = 2048

CHUNK = 64
N_MIXERS = 2
N_HGRN_LAYERS = (DEPTH + N_MIXERS - 1) // N_MIXERS
N_GLA_LAYERS = DEPTH // N_MIXERS
HGRN_HEAD_DIM = 128
HGRN_HEADS = D_MODEL // HGRN_HEAD_DIM
HGRN_WIDTH = HGRN_HEADS * HGRN_HEAD_DIM
GLA_HEADS = 4
GLA_KEY_DIM = D_MODEL // (2 * GLA_HEADS)
GLA_VAL_DIM = D_MODEL // GLA_HEADS
GLA_GATE_RANK = 16
GLA_TAU = 16.0
MEM_TOKENS = 256
MEM_HEADS = 4
MEM_HEAD_DIM = 128
MEM_WIDTH = MEM_HEADS * MEM_HEAD_DIM
D_FF = 2816
ALPHA = (2.0 * DEPTH) ** 0.25
BETA = (8.0 * DEPTH) ** -0.25
LN_EPS = 1e-5
RMS_EPS = 1e-6
GATE_CLAMP = 1.0 - 1e-6
HGRN_SPLITS = (HGRN_WIDTH, 2 * HGRN_WIDTH, 3 * HGRN_WIDTH, 4 * HGRN_WIDTH)
HGRN_IN = 4 * HGRN_WIDTH + MEM_WIDTH
HGRN_MIX_WIDTH = HGRN_WIDTH + MEM_WIDTH
GLA_QK = GLA_HEADS * GLA_KEY_DIM
GLA_V = GLA_HEADS * GLA_VAL_DIM
GLA_SPLITS = (GLA_QK, 2 * GLA_QK, 2 * GLA_QK + GLA_V, 2 * GLA_QK + 2 * GLA_V, 2 * GLA_QK + 2 * GLA_V + GLA_GATE_RANK)
GLA_IN = 2 * GLA_QK + 2 * GLA_V + GLA_GATE_RANK + MEM_WIDTH
GLA_MIX_WIDTH = GLA_V + MEM_WIDTH

kernel_name = 'hybrid_hgrn2_gla_streaming_step'


def layer_norm(x, gain, bias):
    xf = x.astype(jnp.float32)
    mu = jnp.mean(xf, axis=-1, keepdims=True)
    var = jnp.mean(jnp.square(xf - mu), axis=-1, keepdims=True)
    return ((xf - mu) * lax.rsqrt(var + LN_EPS) * gain + bias).astype(x.dtype)


def swiglu(x, w_gate, w_up, w_down):
    return (jax.nn.silu(x @ w_gate) * (x @ w_up)) @ w_down


def gated_linear_attention(q, k, v, log_a, s0):
    B, L, H, K = q.shape
    V = v.shape[-1]
    C = min(CHUNK, L)
    N = L // C
    f32 = jnp.float32

    def chunks(t):
        return t.astype(f32).reshape(B, N, C, H, t.shape[-1]).swapaxes(0, 1)

    causal = jnp.tril(jnp.ones((C, C), dtype=bool))[None, :, :, None, None]

    def step(s, blk):
        qc, kc, vc, gc = blk
        b = jnp.cumsum(gc, axis=1)
        diff = b[:, :, None] - b[:, None]
        decay = jnp.where(causal, jnp.exp(jnp.minimum(diff, 0.0)), 0.0)
        scores = jnp.einsum('bthk,btshk,bshk->bhts', qc, decay, kc)
        o = (jnp.einsum('bhts,bshv->bthv', scores, vc)
             + jnp.einsum('bthk,bhkv->bthv', qc * jnp.exp(b), s))
        b_last = b[:, -1]
        s = (jnp.exp(b_last)[..., None] * s
             + jnp.einsum('bshk,bshv->bhkv', kc * jnp.exp(b_last[:, None] - b), vc))
        return s, o

    s_final, o = lax.scan(step, s0.astype(f32), (chunks(q), chunks(k), chunks(v), chunks(log_a)))
    return o.swapaxes(0, 1).reshape(B, L, H, V), s_final.astype(s0.dtype)


def head_rmsnorm_gate(o, gain, gate, dtype):
    B, L, H, V = o.shape
    o = o * lax.rsqrt(jnp.mean(o * o, axis=-1, keepdims=True) + RMS_EPS)
    return (o.reshape(B, L, H * V) * gain * jax.nn.silu(gate.astype(jnp.float32))).astype(dtype)


def hgrn2_mixer(x, w_in, lower_bound, norm_gain, s0):
    B, L, _ = x.shape
    q, f, i, g, xq = jnp.split(x @ w_in, HGRN_SPLITS, axis=-1)
    heads = lambda t: t.reshape(B, L, HGRN_HEADS, HGRN_HEAD_DIM)
    f32 = f.astype(jnp.float32)
    k = (1.0 - lower_bound) * jax.nn.sigmoid(-f32)
    log_f = jnp.log1p(-jnp.minimum(k, GATE_CLAMP))
    q = jax.nn.silu(q) * HGRN_HEAD_DIM ** -0.5
    o, s = gated_linear_attention(heads(q), heads(k), heads(i), heads(log_f), s0)
    return head_rmsnorm_gate(o, norm_gain, g, x.dtype), xq, s


def gla_mixer(x, w_in, w_gate2, b_gate, norm_gain, s0):
    B, L, _ = x.shape
    q, k, v, r, ga, xq = jnp.split(x @ w_in, GLA_SPLITS, axis=-1)
    heads_k = lambda t: t.reshape(B, L, GLA_HEADS, GLA_KEY_DIM)
    log_a = jax.nn.log_sigmoid((ga @ w_gate2 + b_gate).astype(jnp.float32)) / GLA_TAU
    o, s = gated_linear_attention(heads_k(q) * GLA_KEY_DIM ** -0.5, heads_k(k),
                                  v.reshape(B, L, GLA_HEADS, GLA_VAL_DIM), heads_k(log_a), s0)
    return head_rmsnorm_gate(o, norm_gain, r, x.dtype), xq, s


def memory_attention(xq, mem_k, mem_v):
    B, L, _ = xq.shape
    q = xq.reshape(B, L, MEM_HEADS, MEM_HEAD_DIM)
    s = jnp.einsum('blhd,bmhd->bhlm', q, mem_k).astype(jnp.float32) * MEM_HEAD_DIM ** -0.5
    p = jax.nn.softmax(s, axis=-1).astype(mem_v.dtype)
    return jnp.einsum('bhlm,bmhd->blhd', p, mem_v).reshape(B, L, MEM_WIDTH)


def run_trunk(x, mem_k, mem_v, s_hgrn, s_gla, ffn_w_gate, ffn_w_up, ffn_w_down, ln_gain, ln_bias,
              hgrn_w_in, hgrn_lb_logits, hgrn_norm, hgrn_w_out,
              gla_w_in, gla_w_gate2, gla_b_gate, gla_norm, gla_w_out):
    sm = jax.nn.softmax(hgrn_lb_logits.astype(jnp.float32), axis=0)
    lower_bounds = jnp.cumsum(sm, axis=0) - sm[0]
    new_h, new_g = [], []
    for layer in range(DEPTH):
        x = layer_norm(ALPHA * x + 0.5 * swiglu(x, ffn_w_gate[layer, 0], ffn_w_up[layer, 0], ffn_w_down[layer, 0]),
                       ln_gain[layer, 0], ln_bias[layer, 0])
        j = layer // N_MIXERS
        if layer % N_MIXERS == 0:
            mix, xq, s = hgrn2_mixer(x, hgrn_w_in[j], lower_bounds[j], hgrn_norm[j], s_hgrn[j])
            w_out = hgrn_w_out[j]
            new_h.append(s)
        else:
            mix, xq, s = gla_mixer(x, gla_w_in[j], gla_w_gate2[j], gla_b_gate[j], gla_norm[j], s_gla[j])
            w_out = gla_w_out[j]
            new_g.append(s)
        xo = memory_attention(xq, mem_k[layer], mem_v[layer])
        x = layer_norm(ALPHA * x + jnp.concatenate([mix, xo], axis=-1) @ w_out,
                       ln_gain[layer, 1], ln_bias[layer, 1])
        x = layer_norm(ALPHA * x + 0.5 * swiglu(x, ffn_w_gate[layer, 1], ffn_w_up[layer, 1], ffn_w_down[layer, 1]),
                       ln_gain[layer, 2], ln_bias[layer, 2])
    return x, jnp.stack(new_h), jnp.stack(new_g)


def setup_inputs(seed: int = 0) -> dict:
    key = jax.random.key(seed)
    ks = jax.random.split(key, 26)
    nrm = lambda k, shape, scale: jax.random.normal(k, shape, jnp.float32) * scale
    d = D_MODEL
    hgrn_col_scale = jnp.concatenate([jnp.ones((2 * HGRN_WIDTH,)), BETA * jnp.ones((HGRN_WIDTH,)),
                                      jnp.ones((HGRN_WIDTH + MEM_WIDTH,))])
    gla_col_scale = jnp.concatenate([jnp.ones((2 * GLA_QK,)), BETA * jnp.ones((GLA_V,)),
                                     jnp.ones((GLA_V + GLA_GATE_RANK + MEM_WIDTH,))])
    return {
        'x_prompt': nrm(ks[0], (BATCH, SEQ, d), 1.0),
        'x_sample': nrm(ks[1], (DEC_BATCH, DEC_SEQ, d), 1.0),
        'mem_prompt': nrm(ks[2], (BATCH, MEM_TOKENS, d), 1.0),
        'cache_mem_k': nrm(ks[3], (DEPTH, DEC_BATCH, MEM_TOKENS, MEM_HEADS, MEM_HEAD_DIM), 1.0),
        'cache_mem_v': nrm(ks[4], (DEPTH, DEC_BATCH, MEM_TOKENS, MEM_HEADS, MEM_HEAD_DIM), BETA),
        'state_hgrn': nrm(ks[5], (N_HGRN_LAYERS, DEC_BATCH, HGRN_HEADS, HGRN_HEAD_DIM, HGRN_HEAD_DIM), 0.5),
        'state_gla': nrm(ks[6], (N_GLA_LAYERS, DEC_BATCH, GLA_HEADS, GLA_KEY_DIM, GLA_VAL_DIM), 1.0),
        'ffn_w_gate': nrm(ks[7], (DEPTH, 2, d, D_FF), d ** -0.5),
        'ffn_w_up': nrm(ks[8], (DEPTH, 2, d, D_FF), d ** -0.5),
        'ffn_w_down': nrm(ks[9], (DEPTH, 2, D_FF, d), BETA * D_FF ** -0.5),
        'ln_gain': 1.0 + nrm(ks[10], (DEPTH, 3, d), 0.02),
        'ln_bias': nrm(ks[11], (DEPTH, 3, d), 0.02),
        'hgrn_w_in': nrm(ks[12], (N_HGRN_LAYERS, d, HGRN_IN), d ** -0.5) * hgrn_col_scale,
        'hgrn_lb_logits': nrm(ks[13], (N_HGRN_LAYERS, HGRN_WIDTH), 0.5),
        'hgrn_norm': 1.0 + nrm(ks[14], (N_HGRN_LAYERS, HGRN_WIDTH), 0.02),
        'hgrn_w_out': nrm(ks[15], (N_HGRN_LAYERS, HGRN_MIX_WIDTH, d), BETA * HGRN_MIX_WIDTH ** -0.5),
        'gla_w_in': nrm(ks[16], (N_GLA_LAYERS, d, GLA_IN), d ** -0.5) * gla_col_scale,
        'gla_w_gate2': nrm(ks[17], (N_GLA_LAYERS, GLA_GATE_RANK, GLA_QK), GLA_GATE_RANK ** -0.5),
        'gla_b_gate': nrm(ks[18], (N_GLA_LAYERS, GLA_QK), 0.1),
        'gla_norm': 1.0 + nrm(ks[19], (N_GLA_LAYERS, GLA_V), 0.02),
        'gla_w_out': nrm(ks[20], (N_GLA_LAYERS, GLA_MIX_WIDTH, d), BETA * GLA_MIX_WIDTH ** -0.5),
        'mem_w_k': nrm(ks[21], (DEPTH, d, MEM_WIDTH), d ** -0.5),
        'mem_w_v': nrm(ks[22], (DEPTH, d, MEM_WIDTH), BETA * d ** -0.5),
    }


def reference(x_prompt, x_sample, mem_prompt, cache_mem_k, cache_mem_v, state_hgrn, state_gla,
              ffn_w_gate, ffn_w_up, ffn_w_down, ln_gain, ln_bias,
              hgrn_w_in, hgrn_lb_logits, hgrn_norm, hgrn_w_out,
              gla_w_in, gla_w_gate2, gla_b_gate, gla_norm, gla_w_out, mem_w_k, mem_w_v):
    weights = (ffn_w_gate, ffn_w_up, ffn_w_down, ln_gain, ln_bias,
               hgrn_w_in, hgrn_lb_logits, hgrn_norm, hgrn_w_out,
               gla_w_in, gla_w_gate2, gla_b_gate, gla_norm, gla_w_out)
    b = x_prompt.shape[0]
    mem_k_prompt = jnp.einsum('bmd,ldc->lbmc', mem_prompt, mem_w_k).reshape(DEPTH, b, MEM_TOKENS, MEM_HEADS, MEM_HEAD_DIM)
    mem_v_prompt = jnp.einsum('bmd,ldc->lbmc', mem_prompt, mem_w_v).reshape(DEPTH, b, MEM_TOKENS, MEM_HEADS, MEM_HEAD_DIM)
    zeros_h = jnp.zeros((N_HGRN_LAYERS, b, HGRN_HEADS, HGRN_HEAD_DIM, HGRN_HEAD_DIM), x_prompt.dtype)
    zeros_g = jnp.zeros((N_GLA_LAYERS, b, GLA_HEADS, GLA_KEY_DIM, GLA_VAL_DIM), x_prompt.dtype)
    y_prompt, state_hgrn_prompt, state_gla_prompt = run_trunk(
        x_prompt, mem_k_prompt, mem_v_prompt, zeros_h, zeros_g, *weights)
    y_sample, state_hgrn_sample, state_gla_sample = run_trunk(
        x_sample, cache_mem_k, cache_mem_v, state_hgrn, state_gla, *weights)
    return (y_prompt, y_sample, state_hgrn_prompt, state_gla_prompt, mem_k_prompt, mem_v_prompt,
            state_hgrn_sample, state_gla_sample)
```

```python
import functools

import jax
import jax.numpy as jnp
from jax import lax
from jax.experimental import pallas as pl
from jax.experimental.pallas import tpu as pltpu

F32 = jnp.float32
BF16 = jnp.bfloat16

D_MODEL = 1024
DEPTH = 4
CHUNK = 64
N_MIXERS = 2
HGRN_HEAD_DIM = 128
HGRN_HEADS = D_MODEL // HGRN_HEAD_DIM
HGRN_WIDTH = HGRN_HEADS * HGRN_HEAD_DIM
GLA_HEADS = 4
GLA_KEY_DIM = D_MODEL // (2 * GLA_HEADS)
GLA_VAL_DIM = D_MODEL // GLA_HEADS
GLA_GATE_RANK = 16
GLA_TAU = 16.0
GLA_QK = GLA_HEADS * GLA_KEY_DIM
GLA_V = GLA_HEADS * GLA_VAL_DIM
MEM_TOKENS = 256
MEM_HEADS = 4
MEM_HEAD_DIM = 128
MEM_WIDTH = MEM_HEADS * MEM_HEAD_DIM
D_FF = 2816
ALPHA = (2.0 * DEPTH) ** 0.25
LN_EPS = 1e-5
RMS_EPS = 1e-6
GATE_CLAMP = 1.0 - 1e-6

LANES = 128
GLA_RANK_PAD = LANES
HGRN_Q, HGRN_F, HGRN_I, HGRN_G, HGRN_XQ = (i * HGRN_WIDTH for i in range(5))
HGRN_IN = 4 * HGRN_WIDTH + MEM_WIDTH
GLA_Q, GLA_K, GLA_VO = 0, GLA_QK, 2 * GLA_QK
GLA_R = GLA_VO + GLA_V
GLA_GA = GLA_R + GLA_V
GLA_XQ = GLA_GA + GLA_RANK_PAD
GLA_IN_PAD = GLA_XQ + MEM_WIDTH
MIX_WIDTH = D_MODEL + MEM_WIDTH

VMEM_LIMIT_BYTES = 56 * 1024 * 1024
TOKEN_BLOCK = 512
SEQ_BLOCK = 512

_LEVELS = tuple((h, 2 * h, h - 1) for h in (32, 16, 8, 4, 2, 1))


def _dot(a, b):
    return jnp.dot(a, b, preferred_element_type=F32)


def _dot_nt(a, b):
    return lax.dot_general(a, b, (((1,), (1,)), ((), ())), preferred_element_type=F32)


def _layer_norm(z, gain, bias):
    mu = jnp.mean(z, axis=-1, keepdims=True)
    zc = z - mu
    var = jnp.mean(zc * zc, axis=-1, keepdims=True)
    return zc * lax.rsqrt(var + LN_EPS) * gain + bias


def _silu(x):
    return x * jax.nn.sigmoid(x)


def _ffn_kernel(x_ref, wg_ref, wu_ref, wd_ref, gain_ref, bias_ref, o_ref):
    x = x_ref[...]
    xb = x.astype(BF16)
    g = _dot(xb, wg_ref[...])
    u = _dot(xb, wu_ref[...])
    h = (_silu(g) * u).astype(BF16)
    y = _dot(h, wd_ref[...])
    o_ref[...] = _layer_norm(ALPHA * x + 0.5 * y, gain_ref[...], bias_ref[...])


def _resident(block_shape, index_map):
    return pl.BlockSpec(block_shape, index_map, pipeline_mode=pl.Buffered(1))


def _ffn_call(x, wg, wu, wd, ln_g, ln_b, layer, which, ln_idx):
    t, d = x.shape
    tt = min(TOKEN_BLOCK, t)
    assert t % tt == 0
    widx = lambda i: (layer, which, 0, 0)
    lidx = lambda i: (layer * 3 + ln_idx, 0, 0)
    return pl.pallas_call(
        _ffn_kernel,
        grid=(t // tt,),
        in_specs=[
            pl.BlockSpec((tt, d), lambda i: (i, 0)),
            _resident((None, None, d, D_FF), widx),
            _resident((None, None, d, D_FF), widx),
            _resident((None, None, D_FF, d), widx),
            _resident((None, 1, d), lidx),
            _resident((None, 1, d), lidx),
        ],
        out_specs=pl.BlockSpec((tt, d), lambda i: (i, 0)),
        out_shape=jax.ShapeDtypeStruct((t, d), F32),
        compiler_params=pltpu.CompilerParams(
            dimension_semantics=("parallel",), vmem_limit_bytes=VMEM_LIMIT_BYTES),
        name="swiglu_sublayer",
    )(x, wg, wu, wd, ln_g, ln_b)


def _memkv_kernel(m_ref, wk_ref, wv_ref, k_ref, v_ref):
    mb = m_ref[...].astype(BF16)
    k_ref[...] = _dot(mb, wk_ref[...])
    v_ref[...] = _dot(mb, wv_ref[...])


def _memkv_call(mem, wk, wv):
    t, d = mem.shape
    wspec = pl.BlockSpec((None, d, MEM_WIDTH), lambda l: (l, 0, 0))
    ospec = pl.BlockSpec((None, t, MEM_WIDTH), lambda l: (l, 0, 0))
    oshape = jax.ShapeDtypeStruct((DEPTH, t, MEM_WIDTH), F32)
    return pl.pallas_call(
        _memkv_kernel,
        grid=(DEPTH,),
        in_specs=[_resident((t, d), lambda l: (0, 0)), wspec, wspec],
        out_specs=[ospec, ospec],
        out_shape=[oshape, oshape],
        compiler_params=pltpu.CompilerParams(
            dimension_semantics=("parallel",), vmem_limit_bytes=VMEM_LIMIT_BYTES),
        name="memory_kv_projection",
    )(mem, wk, wv)


def _chunk_masks():
    t = lax.broadcasted_iota(jnp.int32, (CHUNK, CHUNK), 0)
    s = lax.broadcasted_iota(jnp.int32, (CHUNK, CHUNK), 1)
    masks = []
    for h, _, _ in _LEVELS:
        lg = h.bit_length() - 1
        ts, ss = t >> lg, s >> lg
        masks.append((ts == ss + 1) & ((ss & 1) == 0))
    return masks, t == s, (t >= s)


def _cumsum_rows(tri, g):
    g1 = g.astype(BF16)
    r1 = g - g1.astype(F32)
    g2 = r1.astype(BF16)
    g3 = (r1 - g2.astype(F32)).astype(BF16)
    return _dot(tri, g1) + _dot(tri, g2) + _dot(tri, g3)


def _level_exponent(b, g, h, parent, ref_row, row8, odd_rows):
    k = b.shape[-1]
    if h == 1:
        return jnp.where(odd_rows, g, 0.0)
    if parent >= 8:
        b3 = b.reshape(CHUNK // parent, parent, k)
        ref = jnp.broadcast_to(b3[:, ref_row:ref_row + 1, :], b3.shape).reshape(CHUNK, k)
    else:
        b3 = b.reshape(CHUNK // 8, 8, k)
        lo = jnp.broadcast_to(b3[:, ref_row:ref_row + 1, :], b3.shape)
        hi = jnp.broadcast_to(b3[:, 4 + ref_row:5 + ref_row, :], b3.shape)
        ref = jnp.where(row8 < 4, lo, hi).reshape(CHUNK, k)
    d = b - ref
    return jnp.minimum(d, -d)


def _recurrence_chunk(q, k, v, g, st, consts):
    masks, eye, tri, row8, odd_rows = consts
    b = _cumsum_rows(tri, g)
    qb = q.astype(BF16)
    kb = k.astype(BF16)
    a = jnp.where(eye, _dot_nt(qb, kb), 0.0)
    for (h, parent, ref_row), mask in zip(_LEVELS, masks):
        w = jnp.exp(_level_exponent(b, g, h, parent, ref_row, row8, odd_rows))
        p = _dot_nt((q * w).astype(BF16), (k * w).astype(BF16))
        a = a + jnp.where(mask, p, 0.0)
    vb = v.astype(BF16)
    q_in = (q * jnp.exp(b)).astype(BF16)
    o = _dot(a.astype(BF16), vb) + _dot_nt(q_in, st.astype(BF16))
    b_last = b[CHUNK - 1:CHUNK, :]
    k_out = (k * jnp.exp(b_last - b)).astype(BF16)
    st_new = st * jnp.exp(b_last) + _dot(v.T.astype(BF16), k_out)
    return o, st_new


def _recurrence_consts(k):
    masks, eye, causal = _chunk_masks()
    tri = causal.astype(BF16)
    row8 = lax.broadcasted_iota(jnp.int32, (CHUNK // 8, 8, k), 1)
    odd_rows = (lax.broadcasted_iota(jnp.int32, (CHUNK, k), 0) & 1) == 1
    return masks, eye, tri, row8, odd_rows


def _head_norm_gate(o, gain, gate):
    o = o * lax.rsqrt(jnp.mean(o * o, axis=-1, keepdims=True) + RMS_EPS)
    return o * gain * _silu(gate)


def _log_sigmoid(z):
    return -(jnp.maximum(-z, 0.0) + jnp.log1p(jnp.exp(-jnp.abs(z))))


def _hgrn_lower_bound(logits, j):
    m = jnp.max(logits, axis=0, keepdims=True)
    e = jnp.exp(logits - m)
    sm = e / jnp.sum(e, axis=0, keepdims=True)
    lb = jnp.zeros_like(sm[0:1])
    for i in range(1, j + 1):
        lb = lb + sm[i:i + 1]
    return lb


def _memory_and_output(x_ref, proj_scr, mix_scr, mk_ref, mv_ref, wout_ref, lng_ref, lnb_ref, o_ref, xq_off):
    mk = mk_ref[...].astype(BF16)
    mv = mv_ref[...].astype(BF16)
    for h in range(MEM_HEADS):
        cols = slice(h * MEM_HEAD_DIM, (h + 1) * MEM_HEAD_DIM)
        qh = proj_scr[:, pl.ds(xq_off + h * MEM_HEAD_DIM, MEM_HEAD_DIM)].astype(BF16)
        s = _dot_nt(qh, mk[:, cols]) * MEM_HEAD_DIM ** -0.5
        p = jnp.exp(s - jnp.max(s, axis=-1, keepdims=True))
        p = p / jnp.sum(p, axis=-1, keepdims=True)
        xo = _dot(p.astype(BF16), mv[:, cols])
        mix_scr[:, pl.ds(D_MODEL + h * MEM_HEAD_DIM, MEM_HEAD_DIM)] = xo.astype(BF16)
    y = _dot(mix_scr[...], wout_ref[...])
    o_ref[...] = _layer_norm(ALPHA * x_ref[...] + y, lng_ref[...], lnb_ref[...])


def _load_state(s0_ref, st_scr, heads):
    for h in range(heads):
        st_scr[h] = s0_ref[h].T


def _store_state(st_scr, sout_ref, heads):
    for h in range(heads):
        sout_ref[h] = st_scr[h].T


def _hgrn_kernel(x_ref, win_ref, wout_ref, mk_ref, mv_ref, s0_ref, lbl_ref, ng_ref, lng_ref, lnb_ref,
                 o_ref, sout_ref, proj_scr, mix_scr, st_scr, *, layer_j, n_chunks):
    j = pl.program_id(1)

    @pl.when(j == 0)
    def _():
        _load_state(s0_ref, st_scr, HGRN_HEADS)

    proj_scr[...] = _dot(x_ref[...].astype(BF16), win_ref[...])
    key_scale = 1.0 - _hgrn_lower_bound(lbl_ref[...], layer_j)
    consts = _recurrence_consts(HGRN_HEAD_DIM)
    hd = HGRN_HEAD_DIM

    def chunk(c, carry):
        rows = pl.ds(pl.multiple_of(c * CHUNK, CHUNK), CHUNK)
        for h in range(HGRN_HEADS):
            cols = slice(h * hd, (h + 1) * hd)
            q = _silu(proj_scr[rows, pl.ds(HGRN_Q + h * hd, hd)]) * hd ** -0.5
            k = key_scale[:, cols] * jax.nn.sigmoid(-proj_scr[rows, pl.ds(HGRN_F + h * hd, hd)])
            g = jnp.log1p(-jnp.minimum(k, GATE_CLAMP))
            v = proj_scr[rows, pl.ds(HGRN_I + h * hd, hd)]
            o, st_new = _recurrence_chunk(q, k, v, g, st_scr[h], consts)
            st_scr[h] = st_new
            gate = proj_scr[rows, pl.ds(HGRN_G + h * hd, hd)]
            mix_scr[rows, pl.ds(h * hd, hd)] = _head_norm_gate(o, ng_ref[:, cols], gate).astype(BF16)
        return carry

    lax.fori_loop(0, n_chunks, chunk, 0)
    _memory_and_output(x_ref, proj_scr, mix_scr, mk_ref, mv_ref, wout_ref, lng_ref, lnb_ref, o_ref, HGRN_XQ)

    @pl.when(j == pl.num_programs(1) - 1)
    def _():
        _store_state(st_scr, sout_ref, HGRN_HEADS)


def _gla_kernel(x_ref, win_ref, wout_ref, mk_ref, mv_ref, s0_ref, w2_ref, bg_ref, ng_ref, lng_ref, lnb_ref,
                o_ref, sout_ref, proj_scr, mix_scr, st_scr, *, n_chunks):
    j = pl.program_id(1)

    @pl.when(j == 0)
    def _():
        _load_state(s0_ref, st_scr, GLA_HEADS)

    proj_scr[...] = _dot(x_ref[...].astype(BF16), win_ref[...])
    consts = _recurrence_consts(GLA_KEY_DIM)
    kd, vd = GLA_KEY_DIM, GLA_VAL_DIM

    def chunk(c, carry):
        rows = pl.ds(pl.multiple_of(c * CHUNK, CHUNK), CHUNK)
        ga = proj_scr[rows, pl.ds(GLA_GA, GLA_RANK_PAD)].astype(BF16)
        log_a = _log_sigmoid(_dot(ga, w2_ref[...]) + bg_ref[...]) / GLA_TAU
        for h in range(GLA_HEADS):
            q = proj_scr[rows, pl.ds(GLA_Q + h * kd, kd)] * kd ** -0.5
            k = proj_scr[rows, pl.ds(GLA_K + h * kd, kd)]
            v = proj_scr[rows, pl.ds(GLA_VO + h * vd, vd)]
            g = log_a[:, h * kd:(h + 1) * kd]
            o, st_new = _recurrence_chunk(q, k, v, g, st_scr[h], consts)
            st_scr[h] = st_new
            gate = proj_scr[rows, pl.ds(GLA_R + h * vd, vd)]
            mix_scr[rows, pl.ds(h * vd, vd)] = _head_norm_gate(
                o, ng_ref[:, h * vd:(h + 1) * vd], gate).astype(BF16)
        return carry

    lax.fori_loop(0, n_chunks, chunk, 0)
    _memory_and_output(x_ref, proj_scr, mix_scr, mk_ref, mv_ref, wout_ref, lng_ref, lnb_ref, o_ref, GLA_XQ)

    @pl.when(j == pl.num_programs(1) - 1)
    def _():
        _store_state(st_scr, sout_ref, GLA_HEADS)


def _mixer_call(kind, x, w_in, w_out, mem_k, mem_v, s0, extras, ln_g, ln_b, layer, layer_j):
    b, l, d = x.shape
    tt = min(SEQ_BLOCK, l)
    assert l % tt == 0 and tt % CHUNK == 0
    n_in = w_in.shape[-1]
    heads, kd, vd = s0.shape[2:]
    xspec = pl.BlockSpec((None, tt, d), lambda bi, j: (bi, j, 0))
    mspec = pl.BlockSpec((None, None, MEM_TOKENS, MEM_WIDTH), lambda bi, j: (layer, bi, 0, 0))
    s0spec = pl.BlockSpec((None, None, heads, kd, vd), lambda bi, j: (layer_j, bi, 0, 0, 0))
    sspec = pl.BlockSpec((None, heads, kd, vd), lambda bi, j: (bi, 0, 0, 0))
    lnspec = _resident((None, 1, d), lambda bi, j: (layer * 3 + 1, 0, 0))
    if kind == "hgrn":
        body = functools.partial(_hgrn_kernel, layer_j=layer_j, n_chunks=tt // CHUNK)
    else:
        body = functools.partial(_gla_kernel, n_chunks=tt // CHUNK)
    extra_arrays = [a for a, _ in extras]
    extra_specs = [s for _, s in extras]
    return pl.pallas_call(
        body,
        grid=(b, l // tt),
        in_specs=[
            xspec,
            _resident((None, d, n_in), lambda bi, j: (layer_j, 0, 0)),
            _resident((None, MIX_WIDTH, d), lambda bi, j: (layer_j, 0, 0)),
            mspec, mspec, s0spec,
            *extra_specs,
            lnspec, lnspec,
        ],
        out_specs=[xspec, sspec],
        out_shape=[jax.ShapeDtypeStruct((b, l, d), F32), jax.ShapeDtypeStruct(s0.shape[1:], F32)],
        scratch_shapes=[
            pltpu.VMEM((tt, n_in), F32),
            pltpu.VMEM((tt, MIX_WIDTH), BF16),
            pltpu.VMEM((heads, vd, kd), F32),
        ],
        compiler_params=pltpu.CompilerParams(
            dimension_semantics=("parallel", "arbitrary"), vmem_limit_bytes=VMEM_LIMIT_BYTES),
        name=kind + "_mixer_sublayer",
    )(x, w_in, w_out, mem_k, mem_v, s0, *extra_arrays, ln_g, ln_b)


def _run_trunk(x, mem_k, mem_v, s_hgrn, s_gla, w):
    b, l, d = x.shape
    new_h, new_g = [], []
    for layer in range(DEPTH):
        xf = _ffn_call(x.reshape(b * l, d), w["ffn_g"], w["ffn_u"], w["ffn_d"], w["ln_g"], w["ln_b"], layer, 0, 0)
        x = xf.reshape(b, l, d)
        jl = layer // N_MIXERS
        row = lambda bi, j: (jl, 0, 0)
        if layer % N_MIXERS == 0:
            extras = [
                (w["hgrn_lb"], _resident(w["hgrn_lb"].shape, lambda bi, j: (0, 0))),
                (w["hgrn_norm"], _resident((None, 1, HGRN_WIDTH), row)),
            ]
            x, s = _mixer_call("hgrn", x, w["hgrn_in"], w["hgrn_out"], mem_k, mem_v, s_hgrn,
                               extras, w["ln_g"], w["ln_b"], layer, jl)
            new_h.append(s)
        else:
            extras = [
                (w["gla_w2"], _resident((None, GLA_RANK_PAD, GLA_QK), row)),
                (w["gla_bg"], _resident((None, 1, GLA_QK), row)),
                (w["gla_norm"], _resident((None, 1, GLA_V), row)),
            ]
            x, s = _mixer_call("gla", x, w["gla_in"], w["gla_out"], mem_k, mem_v, s_gla,
                               extras, w["ln_g"], w["ln_b"], layer, jl)
            new_g.append(s)
        xf = _ffn_call(x.reshape(b * l, d), w["ffn_g"], w["ffn_u"], w["ffn_d"], w["ln_g"], w["ln_b"], layer, 1, 2)
        x = xf.reshape(b, l, d)
    return x, jnp.stack(new_h), jnp.stack(new_g)


def _prepare_weights(ffn_w_gate, ffn_w_up, ffn_w_down, ln_gain, ln_bias, hgrn_w_in, hgrn_lb_logits, hgrn_norm,
                     hgrn_w_out, gla_w_in, gla_w_gate2, gla_b_gate, gla_norm, gla_w_out):
    n_gla = gla_w_in.shape[0]
    pad_cols = jnp.zeros((n_gla, D_MODEL, GLA_RANK_PAD - GLA_GATE_RANK), gla_w_in.dtype)
    ga_end = 2 * GLA_QK + 2 * GLA_V + GLA_GATE_RANK
    gla_in = jnp.concatenate([gla_w_in[:, :, :ga_end], pad_cols, gla_w_in[:, :, ga_end:]], axis=-1)
    pad_rows = jnp.zeros((n_gla, GLA_RANK_PAD - GLA_GATE_RANK, GLA_QK), gla_w_gate2.dtype)
    gla_w2 = jnp.concatenate([gla_w_gate2, pad_rows], axis=1)
    return {
        "ffn_g": ffn_w_gate.astype(BF16), "ffn_u": ffn_w_up.astype(BF16), "ffn_d": ffn_w_down.astype(BF16),
        "ln_g": ln_gain.reshape(DEPTH * 3, 1, D_MODEL), "ln_b": ln_bias.reshape(DEPTH * 3, 1, D_MODEL),
        "hgrn_in": hgrn_w_in.astype(BF16), "hgrn_lb": hgrn_lb_logits,
        "hgrn_norm": hgrn_norm.reshape(-1, 1, HGRN_WIDTH), "hgrn_out": hgrn_w_out.astype(BF16),
        "gla_in": gla_in.astype(BF16), "gla_w2": gla_w2.astype(BF16),
        "gla_bg": gla_b_gate.reshape(-1, 1, GLA_QK), "gla_norm": gla_norm.reshape(-1, 1, GLA_V),
        "gla_out": gla_w_out.astype(BF16),
    }


def kernel(x_prompt, x_sample, mem_prompt, cache_mem_k, cache_mem_v, state_hgrn, state_gla,
           ffn_w_gate, ffn_w_up, ffn_w_down, ln_gain, ln_bias,
           hgrn_w_in, hgrn_lb_logits, hgrn_norm, hgrn_w_out,
           gla_w_in, gla_w_gate2, gla_b_gate, gla_norm, gla_w_out, mem_w_k, mem_w_v):
    w = _prepare_weights(ffn_w_gate, ffn_w_up, ffn_w_down, ln_gain, ln_bias, hgrn_w_in, hgrn_lb_logits,
                         hgrn_norm, hgrn_w_out, gla_w_in, gla_w_gate2, gla_b_gate, gla_norm, gla_w_out)
    b = x_prompt.shape[0]
    db = x_sample.shape[0]
    mem_k_flat, mem_v_flat = _memkv_call(mem_prompt.reshape(b * MEM_TOKENS, D_MODEL),
                                         mem_w_k.astype(BF16), mem_w_v.astype(BF16))
    mem_k_p = mem_k_flat.reshape(DEPTH, b, MEM_TOKENS, MEM_WIDTH)
    mem_v_p = mem_v_flat.reshape(DEPTH, b, MEM_TOKENS, MEM_WIDTH)
    n_h, n_g = state_hgrn.shape[0], state_gla.shape[0]
    zeros_h = jnp.zeros((n_h, b) + state_hgrn.shape[2:], F32)
    zeros_g = jnp.zeros((n_g, b) + state_gla.shape[2:], F32)
    y_p, sh_p, sg_p = _run_trunk(x_prompt, mem_k_p, mem_v_p, zeros_h, zeros_g, w)
    y_s, sh_s, sg_s = _run_trunk(x_sample,
                                 cache_mem_k.reshape(DEPTH, db, MEM_TOKENS, MEM_WIDTH),
                                 cache_mem_v.reshape(DEPTH, db, MEM_TOKENS, MEM_WIDTH),
                                 state_hgrn, state_gla, w)
    kv_shape = (DEPTH, b, MEM_TOKENS, MEM_HEADS, MEM_HEAD_DIM)
    return (y_p, y_s, sh_p, sg_p, mem_k_flat.reshape(kv_shape), mem_v_flat.reshape(kv_shape), sh_s, sg_s)
```

```python
import functools

import jax
import jax.numpy as jnp
from jax import lax
from jax.experimental import pallas as pl
from jax.experimental.pallas import tpu as pltpu

F32 = jnp.float32
BF16 = jnp.bfloat16

D_MODEL = 1024
DEPTH = 4
CHUNK = 64
N_MIXERS = 2
HGRN_HEAD_DIM = 128
HGRN_HEADS = D_MODEL // HGRN_HEAD_DIM
HGRN_WIDTH = HGRN_HEADS * HGRN_HEAD_DIM
GLA_HEADS = 4
GLA_KEY_DIM = D_MODEL // (2 * GLA_HEADS)
GLA_VAL_DIM = D_MODEL // GLA_HEADS
GLA_GATE_RANK = 16
GLA_TAU = 16.0
GLA_QK = GLA_HEADS * GLA_KEY_DIM
GLA_V = GLA_HEADS * GLA_VAL_DIM
MEM_TOKENS = 256
MEM_HEADS = 4
MEM_HEAD_DIM = 128
MEM_WIDTH = MEM_HEADS * MEM_HEAD_DIM
D_FF = 2816
ALPHA = (2.0 * DEPTH) ** 0.25
LN_EPS = 1e-5
RMS_EPS = 1e-6
GATE_CLAMP = 1.0 - 1e-6

LANES = 128
GLA_RANK_PAD = LANES
HGRN_Q, HGRN_F, HGRN_I, HGRN_G, HGRN_XQ = (i * HGRN_WIDTH for i in range(5))
HGRN_IN = 4 * HGRN_WIDTH + MEM_WIDTH
GLA_Q, GLA_K, GLA_VO = 0, GLA_QK, 2 * GLA_QK
GLA_R = GLA_VO + GLA_V
GLA_GA = GLA_R + GLA_V
GLA_XQ = GLA_GA + GLA_RANK_PAD
GLA_IN_PAD = GLA_XQ + MEM_WIDTH
MIX_WIDTH = D_MODEL + MEM_WIDTH

VMEM_LIMIT_BYTES = 56 * 1024 * 1024
TOKEN_BLOCK = 512
SEQ_BLOCK = 512

SUBLANES = 8
LOG2E = 1.4426950408889634
_LEVEL_HALVES = (32, 16, 8, 4, 2, 1)


def _dot(a, b):
    return jnp.dot(a, b, preferred_element_type=F32)


def _dot_nt(a, b):
    return lax.dot_general(a, b, (((1,), (1,)), ((), ())), preferred_element_type=F32)


def _layer_norm(z, gain, bias):
    mu = jnp.mean(z, axis=-1, keepdims=True)
    zc = z - mu
    var = jnp.mean(zc * zc, axis=-1, keepdims=True)
    return zc * lax.rsqrt(var + LN_EPS) * gain + bias


def _silu(x):
    return x * jax.nn.sigmoid(x)


def _ffn_kernel(x_ref, wg_ref, wu_ref, wd_ref, gain_ref, bias_ref, o_ref):
    x = x_ref[...]
    xb = x.astype(BF16)
    g = _dot(xb, wg_ref[...])
    u = _dot(xb, wu_ref[...])
    h = (_silu(g) * u).astype(BF16)
    y = _dot(h, wd_ref[...])
    o_ref[...] = _layer_norm(ALPHA * x + 0.5 * y, gain_ref[...], bias_ref[...])


def _resident(block_shape, index_map):
    return pl.BlockSpec(block_shape, index_map, pipeline_mode=pl.Buffered(1))


def _ffn_call(x, wg, wu, wd, ln_g, ln_b, layer, which, ln_idx):
    t, d = x.shape
    tt = min(TOKEN_BLOCK, t)
    assert t % tt == 0
    widx = lambda i: (layer, which, 0, 0)
    lidx = lambda i: (layer * 3 + ln_idx, 0, 0)
    return pl.pallas_call(
        _ffn_kernel,
        grid=(t // tt,),
        in_specs=[
            pl.BlockSpec((tt, d), lambda i: (i, 0)),
            _resident((None, None, d, D_FF), widx),
            _resident((None, None, d, D_FF), widx),
            _resident((None, None, D_FF, d), widx),
            _resident((None, 1, d), lidx),
            _resident((None, 1, d), lidx),
        ],
        out_specs=pl.BlockSpec((tt, d), lambda i: (i, 0)),
        out_shape=jax.ShapeDtypeStruct((t, d), F32),
        compiler_params=pltpu.CompilerParams(
            dimension_semantics=("parallel",), vmem_limit_bytes=VMEM_LIMIT_BYTES),
        name="swiglu_sublayer",
    )(x, wg, wu, wd, ln_g, ln_b)


def _memkv_kernel(m_ref, wk_ref, wv_ref, k_ref, v_ref):
    mb = m_ref[...].astype(BF16)
    k_ref[...] = _dot(mb, wk_ref[...])
    v_ref[...] = _dot(mb, wv_ref[...])


def _memkv_call(mem, wk, wv):
    t, d = mem.shape
    wspec = pl.BlockSpec((None, d, MEM_WIDTH), lambda l: (l, 0, 0))
    ospec = pl.BlockSpec((None, t, MEM_WIDTH), lambda l: (l, 0, 0))
    oshape = jax.ShapeDtypeStruct((DEPTH, t, MEM_WIDTH), F32)
    return pl.pallas_call(
        _memkv_kernel,
        grid=(DEPTH,),
        in_specs=[_resident((t, d), lambda l: (0, 0)), wspec, wspec],
        out_specs=[ospec, ospec],
        out_shape=[oshape, oshape],
        compiler_params=pltpu.CompilerParams(
            dimension_semantics=("parallel",), vmem_limit_bytes=VMEM_LIMIT_BYTES),
        name="memory_kv_projection",
    )(mem, wk, wv)


def _chunk_consts(kd):
    t = lax.broadcasted_iota(jnp.int32, (CHUNK, CHUNK), 0)
    s = lax.broadcasted_iota(jnp.int32, (CHUNK, CHUNK), 1)
    level_masks = []
    for h in _LEVEL_HALVES:
        lg = h.bit_length() - 1
        ts, ss = t >> lg, s >> lg
        level_masks.append((ts == ss + 1) & ((ss & 1) == 0))
    sub = lax.broadcasted_iota(jnp.int32, (1, SUBLANES, kd), 1)
    return {"levels": level_masks, "eye": t == s, "tri": (t >= s).astype(BF16), "sub": sub}


def _cumsum_rows(tri, g):
    g1 = g.astype(BF16)
    g2 = (g - g1.astype(F32)).astype(BF16)
    return _dot(tri, g1) + _dot(tri, g2)


def _neg_abs(x):
    bits = lax.bitcast_convert_type(x, jnp.int32) | jnp.int32(-2 ** 31)
    return lax.bitcast_convert_type(bits, F32)


def _level_operand(q, k, b2, g2, h, sub):
    kd = q.shape[-1]
    parent = 2 * h
    if parent >= 2 * SUBLANES:
        n = CHUNK // parent
        b4 = b2.reshape(n, parent, kd)
        x = b4 - b4[:, h - 1:h, :]
        src = jnp.concatenate([k.reshape(n, parent, kd)[:, :h, :], q.reshape(n, parent, kd)[:, h:, :]], axis=1)
        return (src * jnp.exp2(_neg_abs(x))).astype(BF16).reshape(CHUNK, kd)
    n = CHUNK // SUBLANES
    b3, q3, k3 = (a.reshape(n, SUBLANES, kd) for a in (b2, q, k))
    upper = (sub & h) != 0
    if h == 4:
        x = _neg_abs(b3 - b3[:, 3:4, :])
    elif h == 2:
        x = _neg_abs(b3 - jnp.where(sub < 4, b3[:, 1:2, :], b3[:, 5:6, :]))
    else:
        x = jnp.where(upper, g2.reshape(n, SUBLANES, kd), 0.0)
    return (jnp.where(upper, q3, k3) * jnp.exp2(x)).astype(BF16).reshape(CHUNK, kd)


def _intra_chunk_scores(q, k, b2, g2, consts):
    a = jnp.where(consts["eye"], _dot_nt(q.astype(BF16), k.astype(BF16)), 0.0)
    for h, mask in zip(_LEVEL_HALVES, consts["levels"]):
        z = _level_operand(q, k, b2, g2, h, consts["sub"])
        a = jnp.where(mask, _dot_nt(z, z), a)
    return a.astype(BF16)


def _recurrence_chunk(q, k, g, load_v, st_scr, consts, heads, kd):
    b = _cumsum_rows(consts["tri"], g)
    b2 = b * LOG2E
    g2 = g * LOG2E
    b2_last = b2[CHUNK - 1:CHUNK, :]
    q_in = (q * jnp.exp2(b2)).astype(BF16)
    k_out = (k * jnp.exp2(b2_last - b2)).astype(BF16)
    decay = jnp.exp2(b2_last)
    scores = []
    for h in range(heads):
        cs = slice(h * kd, (h + 1) * kd)
        scores.append(_intra_chunk_scores(q[:, cs], k[:, cs], b2[:, cs], g2[:, cs], consts))
    outs = []
    for h in range(heads):
        cs = slice(h * kd, (h + 1) * kd)
        v = load_v(h)
        st = st_scr[h]
        outs.append(_dot(scores[h], v.astype(BF16)) + _dot_nt(q_in[:, cs], st.astype(BF16)))
        st_scr[h] = st * decay[:, cs] + _dot(v.T.astype(BF16), k_out[:, cs])
    return outs


def _head_norm_gate(o, gain, gate):
    o = o * lax.rsqrt(jnp.mean(o * o, axis=-1, keepdims=True) + RMS_EPS)
    return o * gain * _silu(gate)


def _log_sigmoid(z):
    return -(jnp.maximum(-z, 0.0) + jnp.log1p(jnp.exp(-jnp.abs(z))))


def _hgrn_lower_bound(logits, j):
    m = jnp.max(logits, axis=0, keepdims=True)
    e = jnp.exp(logits - m)
    sm = e / jnp.sum(e, axis=0, keepdims=True)
    lb = jnp.zeros_like(sm[0:1])
    for i in range(1, j + 1):
        lb = lb + sm[i:i + 1]
    return lb


def _memory_and_output(x_ref, proj_scr, mix_scr, mk_ref, mv_ref, wout_ref, lng_ref, lnb_ref, o_ref, xq_off):
    mk = mk_ref[...].astype(BF16)
    mv = mv_ref[...].astype(BF16)
    for h in range(MEM_HEADS):
        cols = slice(h * MEM_HEAD_DIM, (h + 1) * MEM_HEAD_DIM)
        qh = proj_scr[:, pl.ds(xq_off + h * MEM_HEAD_DIM, MEM_HEAD_DIM)].astype(BF16)
        s = _dot_nt(qh, mk[:, cols]) * MEM_HEAD_DIM ** -0.5
        p = jnp.exp(s - jnp.max(s, axis=-1, keepdims=True))
        p = p / jnp.sum(p, axis=-1, keepdims=True)
        xo = _dot(p.astype(BF16), mv[:, cols])
        mix_scr[:, pl.ds(D_MODEL + h * MEM_HEAD_DIM, MEM_HEAD_DIM)] = xo.astype(BF16)
    y = _dot(mix_scr[...], wout_ref[...])
    o_ref[...] = _layer_norm(ALPHA * x_ref[...] + y, lng_ref[...], lnb_ref[...])


def _load_state(s0_ref, st_scr, heads):
    for h in range(heads):
        st_scr[h] = s0_ref[h].T


def _store_state(st_scr, sout_ref, heads):
    for h in range(heads):
        sout_ref[h] = st_scr[h].T


def _hgrn_kernel(x_ref, win_ref, wout_ref, mk_ref, mv_ref, s0_ref, lbl_ref, ng_ref, lng_ref, lnb_ref,
                 o_ref, sout_ref, proj_scr, mix_scr, st_scr, *, layer_j, n_chunks):
    j = pl.program_id(1)

    @pl.when(j == 0)
    def _():
        _load_state(s0_ref, st_scr, HGRN_HEADS)

    proj_scr[...] = _dot(x_ref[...].astype(BF16), win_ref[...])
    key_scale = 1.0 - _hgrn_lower_bound(lbl_ref[...], layer_j)
    consts = _chunk_consts(HGRN_HEAD_DIM)
    hd = HGRN_HEAD_DIM

    def chunk(c, carry):
        rows = pl.ds(pl.multiple_of(c * CHUNK, CHUNK), CHUNK)
        q = _silu(proj_scr[rows, pl.ds(HGRN_Q, HGRN_WIDTH)]) * hd ** -0.5
        k = key_scale * jax.nn.sigmoid(-proj_scr[rows, pl.ds(HGRN_F, HGRN_WIDTH)])
        g = jnp.log(1.0 - jnp.minimum(k, GATE_CLAMP))
        load_v = lambda h: proj_scr[rows, pl.ds(HGRN_I + h * hd, hd)]
        outs = _recurrence_chunk(q, k, g, load_v, st_scr, consts, HGRN_HEADS, hd)
        for h, o in enumerate(outs):
            gate = proj_scr[rows, pl.ds(HGRN_G + h * hd, hd)]
            mix_scr[rows, pl.ds(h * hd, hd)] = _head_norm_gate(
                o, ng_ref[:, h * hd:(h + 1) * hd], gate).astype(BF16)
        return carry

    lax.fori_loop(0, n_chunks, chunk, 0)
    _memory_and_output(x_ref, proj_scr, mix_scr, mk_ref, mv_ref, wout_ref, lng_ref, lnb_ref, o_ref, HGRN_XQ)

    @pl.when(j == pl.num_programs(1) - 1)
    def _():
        _store_state(st_scr, sout_ref, HGRN_HEADS)


def _gla_kernel(x_ref, win_ref, wout_ref, mk_ref, mv_ref, s0_ref, w2_ref, bg_ref, ng_ref, lng_ref, lnb_ref,
                o_ref, sout_ref, proj_scr, mix_scr, st_scr, *, n_chunks):
    j = pl.program_id(1)

    @pl.when(j == 0)
    def _():
        _load_state(s0_ref, st_scr, GLA_HEADS)

    proj_scr[...] = _dot(x_ref[...].astype(BF16), win_ref[...])
    consts = _chunk_consts(GLA_KEY_DIM)
    kd, vd = GLA_KEY_DIM, GLA_VAL_DIM

    def chunk(c, carry):
        rows = pl.ds(pl.multiple_of(c * CHUNK, CHUNK), CHUNK)
        ga = proj_scr[rows, pl.ds(GLA_GA, GLA_RANK_PAD)].astype(BF16)
        g = _log_sigmoid(_dot(ga, w2_ref[...]) + bg_ref[...]) / GLA_TAU
        q = proj_scr[rows, pl.ds(GLA_Q, GLA_QK)] * kd ** -0.5
        k = proj_scr[rows, pl.ds(GLA_K, GLA_QK)]
        load_v = lambda h: proj_scr[rows, pl.ds(GLA_VO + h * vd, vd)]
        outs = _recurrence_chunk(q, k, g, load_v, st_scr, consts, GLA_HEADS, kd)
        for h, o in enumerate(outs):
            gate = proj_scr[rows, pl.ds(GLA_R + h * vd, vd)]
            mix_scr[rows, pl.ds(h * vd, vd)] = _head_norm_gate(
                o, ng_ref[:, h * vd:(h + 1) * vd], gate).astype(BF16)
        return carry

    lax.fori_loop(0, n_chunks, chunk, 0)
    _memory_and_output(x_ref, proj_scr, mix_scr, mk_ref, mv_ref, wout_ref, lng_ref, lnb_ref, o_ref, GLA_XQ)

    @pl.when(j == pl.num_programs(1) - 1)
    def _():
        _store_state(st_scr, sout_ref, GLA_HEADS)


def _mixer_call(kind, x, w_in, w_out, mem_k, mem_v, s0, extras, ln_g, ln_b, layer, layer_j):
    b, l, d = x.shape
    tt = min(SEQ_BLOCK, l)
    assert l % tt == 0 and tt % CHUNK == 0
    n_in = w_in.shape[-1]
    heads, kd, vd = s0.shape[2:]
    xspec = pl.BlockSpec((None, tt, d), lambda bi, j: (bi, j, 0))
    mspec = pl.BlockSpec((None, None, MEM_TOKENS, MEM_WIDTH), lambda bi, j: (layer, bi, 0, 0))
    s0spec = pl.BlockSpec((None, None, heads, kd, vd), lambda bi, j: (layer_j, bi, 0, 0, 0))
    sspec = pl.BlockSpec((None, heads, kd, vd), lambda bi, j: (bi, 0, 0, 0))
    lnspec = _resident((None, 1, d), lambda bi, j: (layer * 3 + 1, 0, 0))
    if kind == "hgrn":
        body = functools.partial(_hgrn_kernel, layer_j=layer_j, n_chunks=tt // CHUNK)
    else:
        body = functools.partial(_gla_kernel, n_chunks=tt // CHUNK)
    extra_arrays = [a for a, _ in extras]
    extra_specs = [s for _, s in extras]
    return pl.pallas_call(
        body,
        grid=(b, l // tt),
        in_specs=[
            xspec,
            _resident((None, d, n_in), lambda bi, j: (layer_j, 0, 0)),
            _resident((None, MIX_WIDTH, d), lambda bi, j: (layer_j, 0, 0)),
            mspec, mspec, s0spec,
            *extra_specs,
            lnspec, lnspec,
        ],
        out_specs=[xspec, sspec],
        out_shape=[jax.ShapeDtypeStruct((b, l, d), F32), jax.ShapeDtypeStruct(s0.shape[1:], F32)],
        scratch_shapes=[
            pltpu.VMEM((tt, n_in), F32),
            pltpu.VMEM((tt, MIX_WIDTH), BF16),
            pltpu.VMEM((heads, vd, kd), F32),
        ],
        compiler_params=pltpu.CompilerParams(
            dimension_semantics=("parallel", "arbitrary"), vmem_limit_bytes=VMEM_LIMIT_BYTES),
        name=kind + "_mixer_sublayer",
    )(x, w_in, w_out, mem_k, mem_v, s0, *extra_arrays, ln_g, ln_b)


def _run_trunk(x, mem_k, mem_v, s_hgrn, s_gla, w):
    b, l, d = x.shape
    new_h, new_g = [], []
    for layer in range(DEPTH):
        xf = _ffn_call(x.reshape(b * l, d), w["ffn_g"], w["ffn_u"], w["ffn_d"], w["ln_g"], w["ln_b"], layer, 0, 0)
        x = xf.reshape(b, l, d)
        jl = layer // N_MIXERS
        row = lambda bi, j: (jl, 0, 0)
        if layer % N_MIXERS == 0:
            extras = [
                (w["hgrn_lb"], _resident(w["hgrn_lb"].shape, lambda bi, j: (0, 0))),
                (w["hgrn_norm"], _resident((None, 1, HGRN_WIDTH), row)),
            ]
            x, s = _mixer_call("hgrn", x, w["hgrn_in"], w["hgrn_out"], mem_k, mem_v, s_hgrn,
                               extras, w["ln_g"], w["ln_b"], layer, jl)
            new_h.append(s)
        else:
            extras = [
                (w["gla_w2"], _resident((None, GLA_RANK_PAD, GLA_QK), row)),
                (w["gla_bg"], _resident((None, 1, GLA_QK), row)),
                (w["gla_norm"], _resident((None, 1, GLA_V), row)),
            ]
            x, s = _mixer_call("gla", x, w["gla_in"], w["gla_out"], mem_k, mem_v, s_gla,
                               extras, w["ln_g"], w["ln_b"], layer, jl)
            new_g.append(s)
        xf = _ffn_call(x.reshape(b * l, d), w["ffn_g"], w["ffn_u"], w["ffn_d"], w["ln_g"], w["ln_b"], layer, 1, 2)
        x = xf.reshape(b, l, d)
    return x, jnp.stack(new_h), jnp.stack(new_g)


def _prepare_weights(ffn_w_gate, ffn_w_up, ffn_w_down, ln_gain, ln_bias, hgrn_w_in, hgrn_lb_logits, hgrn_norm,
                     hgrn_w_out, gla_w_in, gla_w_gate2, gla_b_gate, gla_norm, gla_w_out):
    n_gla = gla_w_in.shape[0]
    pad_cols = jnp.zeros((n_gla, D_MODEL, GLA_RANK_PAD - GLA_GATE_RANK), gla_w_in.dtype)
    ga_end = 2 * GLA_QK + 2 * GLA_V + GLA_GATE_RANK
    gla_in = jnp.concatenate([gla_w_in[:, :, :ga_end], pad_cols, gla_w_in[:, :, ga_end:]], axis=-1)
    pad_rows = jnp.zeros((n_gla, GLA_RANK_PAD - GLA_GATE_RANK, GLA_QK), gla_w_gate2.dtype)
    gla_w2 = jnp.concatenate([gla_w_gate2, pad_rows], axis=1)
    return {
        "ffn_g": ffn_w_gate.astype(BF16), "ffn_u": ffn_w_up.astype(BF16), "ffn_d": ffn_w_down.astype(BF16),
        "ln_g": ln_gain.reshape(DEPTH * 3, 1, D_MODEL), "ln_b": ln_bias.reshape(DEPTH * 3, 1, D_MODEL),
        "hgrn_in": hgrn_w_in.astype(BF16), "hgrn_lb": hgrn_lb_logits,
        "hgrn_norm": hgrn_norm.reshape(-1, 1, HGRN_WIDTH), "hgrn_out": hgrn_w_out.astype(BF16),
        "gla_in": gla_in.astype(BF16), "gla_w2": gla_w2.astype(BF16),
        "gla_bg": gla_b_gate.reshape(-1, 1, GLA_QK), "gla_norm": gla_norm.reshape(-1, 1, GLA_V),
        "gla_out": gla_w_out.astype(BF16),
    }


def kernel(x_prompt, x_sample, mem_prompt, cache_mem_k, cache_mem_v, state_hgrn, state_gla,
           ffn_w_gate, ffn_w_up, ffn_w_down, ln_gain, ln_bias,
           hgrn_w_in, hgrn_lb_logits, hgrn_norm, hgrn_w_out,
           gla_w_in, gla_w_gate2, gla_b_gate, gla_norm, gla_w_out, mem_w_k, mem_w_v):
    w = _prepare_weights(ffn_w_gate, ffn_w_up, ffn_w_down, ln_gain, ln_bias, hgrn_w_in, hgrn_lb_logits,
                         hgrn_norm, hgrn_w_out, gla_w_in, gla_w_gate2, gla_b_gate, gla_norm, gla_w_out)
    b = x_prompt.shape[0]
    db = x_sample.shape[0]
    mem_k_flat, mem_v_flat = _memkv_call(mem_prompt.reshape(b * MEM_TOKENS, D_MODEL),
                                         mem_w_k.astype(BF16), mem_w_v.astype(BF16))
    mem_k_p = mem_k_flat.reshape(DEPTH, b, MEM_TOKENS, MEM_WIDTH)
    mem_v_p = mem_v_flat.reshape(DEPTH, b, MEM_TOKENS, MEM_WIDTH)
    n_h, n_g = state_hgrn.shape[0], state_gla.shape[0]
    zeros_h = jnp.zeros((n_h, b) + state_hgrn.shape[2:], F32)
    zeros_g = jnp.zeros((n_g, b) + state_gla.shape[2:], F32)
    y_p, sh_p, sg_p = _run_trunk(x_prompt, mem_k_p, mem_v_p, zeros_h, zeros_g, w)
    y_s, sh_s, sg_s = _run_trunk(x_sample,
                                 cache_mem_k.reshape(DEPTH, db, MEM_TOKENS, MEM_WIDTH),
                                 cache_mem_v.reshape(DEPTH, db, MEM_TOKENS, MEM_WIDTH),
                                 state_hgrn, state_gla, w)
    kv_shape = (DEPTH, b, MEM_TOKENS, MEM_HEADS, MEM_HEAD_DIM)
    return (y_p, y_s, sh_p, sg_p, mem_k_flat.reshape(kv_shape), mem_v_flat.reshape(kv_shape), sh_s, sg_s)
```

```python
import functools

import jax
import jax.numpy as jnp
from jax import lax
from jax.experimental import pallas as pl
from jax.experimental.pallas import tpu as pltpu

F32 = jnp.float32
BF16 = jnp.bfloat16

D_MODEL = 1024
DEPTH = 4
CHUNK = 64
N_MIXERS = 2
HGRN_HEAD_DIM = 128
HGRN_HEADS = D_MODEL // HGRN_HEAD_DIM
HGRN_WIDTH = HGRN_HEADS * HGRN_HEAD_DIM
GLA_HEADS = 4
GLA_KEY_DIM = D_MODEL // (2 * GLA_HEADS)
GLA_VAL_DIM = D_MODEL // GLA_HEADS
GLA_GATE_RANK = 16
GLA_TAU = 16.0
GLA_QK = GLA_HEADS * GLA_KEY_DIM
GLA_V = GLA_HEADS * GLA_VAL_DIM
MEM_TOKENS = 256
MEM_HEADS = 4
MEM_HEAD_DIM = 128
MEM_WIDTH = MEM_HEADS * MEM_HEAD_DIM
D_FF = 2816
ALPHA = (2.0 * DEPTH) ** 0.25
LN_EPS = 1e-5
RMS_EPS = 1e-6
GATE_CLAMP = 1.0 - 1e-6

LANES = 128
SUBLANES = 8
LOG2E = 1.4426950408889634
GLA_RANK_PAD = LANES
HGRN_Q, HGRN_F, HGRN_I, HGRN_G, HGRN_XQ = (i * HGRN_WIDTH for i in range(5))
HGRN_IN = 4 * HGRN_WIDTH + MEM_WIDTH
GLA_Q, GLA_K, GLA_VO = 0, GLA_QK, 2 * GLA_QK
GLA_R = GLA_VO + GLA_V
GLA_GA = GLA_R + GLA_V
GLA_XQ = GLA_GA + GLA_RANK_PAD
GLA_IN_PAD = GLA_XQ + MEM_WIDTH
MIX_WIDTH = D_MODEL + MEM_WIDTH

VMEM_LIMIT_BYTES = 56 * 1024 * 1024
TOKEN_BLOCK = 1024
SEQ_BLOCK = 512
PROJ_PIECE = 512

_LEVEL_HALVES = (32, 16, 8, 4, 2, 1)


def _dot(a, b):
    return jnp.dot(a, b, preferred_element_type=F32)


def _dot_nt(a, b):
    return lax.dot_general(a, b, (((1,), (1,)), ((), ())), preferred_element_type=F32)


def _layer_norm(z, gain, bias):
    mu = jnp.mean(z, axis=-1, keepdims=True)
    zc = z - mu
    var = jnp.mean(zc * zc, axis=-1, keepdims=True)
    return zc * lax.rsqrt(var + LN_EPS) * gain + bias


def _silu(x):
    return x * jax.nn.sigmoid(x)


def _resident(block_shape, index_map):
    return pl.BlockSpec(block_shape, index_map, pipeline_mode=pl.Buffered(1))


def _ffn_kernel(x_ref, wg_ref, wu_ref, wd_ref, gain_ref, bias_ref, o_ref):
    x = x_ref[...]
    xb = x.astype(BF16)
    g = _dot(xb, wg_ref[...])
    u = _dot(xb, wu_ref[...])
    h = (_silu(g) * u).astype(BF16)
    y = _dot(h, wd_ref[...])
    o_ref[...] = _layer_norm(ALPHA * x + 0.5 * y, gain_ref[...], bias_ref[...])


def _ffn_call(x, wg, wu, wd, ln_g, ln_b, layer, which, ln_idx):
    t, d = x.shape
    tt = min(TOKEN_BLOCK, t)
    assert t % tt == 0
    widx = lambda i: (layer, which, 0, 0)
    lidx = lambda i: (layer * 3 + ln_idx, 0, 0)
    return pl.pallas_call(
        _ffn_kernel,
        grid=(t // tt,),
        in_specs=[
            pl.BlockSpec((tt, d), lambda i: (i, 0)),
            _resident((None, None, d, D_FF), widx),
            _resident((None, None, d, D_FF), widx),
            _resident((None, None, D_FF, d), widx),
            _resident((None, 1, d), lidx),
            _resident((None, 1, d), lidx),
        ],
        out_specs=pl.BlockSpec((tt, d), lambda i: (i, 0)),
        out_shape=jax.ShapeDtypeStruct((t, d), F32),
        compiler_params=pltpu.CompilerParams(
            dimension_semantics=("parallel",), vmem_limit_bytes=VMEM_LIMIT_BYTES),
        name="swiglu_sublayer",
    )(x, wg, wu, wd, ln_g, ln_b)


def _memkv_kernel(m_ref, wk_ref, wv_ref, k_ref, v_ref):
    mb = m_ref[...].astype(BF16)
    k_ref[...] = _dot(mb, wk_ref[...])
    v_ref[...] = _dot(mb, wv_ref[...])


def _memkv_call(mem, wk, wv):
    t, d = mem.shape
    wspec = pl.BlockSpec((None, d, MEM_WIDTH), lambda l: (l, 0, 0))
    ospec = pl.BlockSpec((None, t, MEM_WIDTH), lambda l: (l, 0, 0))
    oshape = jax.ShapeDtypeStruct((DEPTH, t, MEM_WIDTH), F32)
    return pl.pallas_call(
        _memkv_kernel,
        grid=(DEPTH,),
        in_specs=[_resident((t, d), lambda l: (0, 0)), wspec, wspec],
        out_specs=[ospec, ospec],
        out_shape=[oshape, oshape],
        compiler_params=pltpu.CompilerParams(
            dimension_semantics=("parallel",), vmem_limit_bytes=VMEM_LIMIT_BYTES),
        name="memory_kv_projection",
    )(mem, wk, wv)


def _chunk_consts(kd):
    t = lax.broadcasted_iota(jnp.int32, (CHUNK, CHUNK), 0)
    s = lax.broadcasted_iota(jnp.int32, (CHUNK, CHUNK), 1)
    level_masks = []
    for h in _LEVEL_HALVES:
        lg = h.bit_length() - 1
        ts, ss = t >> lg, s >> lg
        level_masks.append((ts == ss + 1) & ((ss & 1) == 0))
    sub = lax.broadcasted_iota(jnp.int32, (1, SUBLANES, kd), 1)
    return {"levels": level_masks, "eye": t == s, "tri": (t >= s).astype(BF16), "sub": sub}


def _cumsum_rows(tri, g):
    g1 = g.astype(BF16)
    g2 = (g - g1.astype(F32)).astype(BF16)
    return _dot(tri, g1) + _dot(tri, g2)


def _neg_abs(x):
    bits = lax.bitcast_convert_type(x, jnp.int32) | jnp.int32(-2 ** 31)
    return lax.bitcast_convert_type(bits, F32)


def _level_operand(q, k, b2, g2, h, sub):
    kd = q.shape[-1]
    parent = 2 * h
    if parent >= 2 * SUBLANES:
        n = CHUNK // parent
        b4 = b2.reshape(n, parent, kd)
        x = b4 - b4[:, h - 1:h, :]
        src = jnp.concatenate([k.reshape(n, parent, kd)[:, :h, :], q.reshape(n, parent, kd)[:, h:, :]], axis=1)
        return (src * jnp.exp2(_neg_abs(x))).astype(BF16).reshape(CHUNK, kd)
    n = CHUNK // SUBLANES
    b3, q3, k3 = (a.reshape(n, SUBLANES, kd) for a in (b2, q, k))
    upper = (sub & h) != 0
    if h == 4:
        x = _neg_abs(b3 - b3[:, 3:4, :])
    elif h == 2:
        x = _neg_abs(b3 - jnp.where(sub < 4, b3[:, 1:2, :], b3[:, 5:6, :]))
    else:
        x = jnp.where(upper, g2.reshape(n, SUBLANES, kd), 0.0)
    return (jnp.where(upper, q3, k3) * jnp.exp2(x)).astype(BF16).reshape(CHUNK, kd)


def _intra_chunk_scores(q, k, b2, g2, consts):
    a = jnp.where(consts["eye"], _dot_nt(q.astype(BF16), k.astype(BF16)), 0.0)
    for h, mask in zip(_LEVEL_HALVES, consts["levels"]):
        z = _level_operand(q, k, b2, g2, h, consts["sub"])
        a = jnp.where(mask, _dot_nt(z, z), a)
    return a.astype(BF16)


def _stage_head(stage, h, q, k, g, tri):
    stage_f, stage_b, stage_d = stage
    kd = q.shape[-1]
    cs = pl.ds(h * kd, kd)
    b2 = _cumsum_rows(tri, g) * LOG2E
    b2_last = b2[CHUNK - 1:CHUNK, :]
    stage_f[0, :, cs] = q
    stage_f[1, :, cs] = k
    stage_f[2, :, cs] = b2
    stage_f[3, :, cs] = g * LOG2E
    stage_b[0, :, cs] = (q * jnp.exp2(b2)).astype(BF16)
    stage_b[1, :, cs] = (k * jnp.exp2(b2_last - b2)).astype(BF16)
    stage_d[:, cs] = jnp.exp2(b2_last)


def _recurrence_chunk(stage, load_v, st_scr, consts, heads, kd, after_scores, between):
    stage_f, stage_b, stage_d = stage
    scores = []
    for h in range(heads):
        cs = pl.ds(h * kd, kd)
        scores.append(_intra_chunk_scores(stage_f[0, :, cs], stage_f[1, :, cs], stage_f[2, :, cs],
                                          stage_f[3, :, cs], consts))
    carried = [(stage_b[0, :, pl.ds(h * kd, kd)], stage_b[1, :, pl.ds(h * kd, kd)], stage_d[:, pl.ds(h * kd, kd)])
               for h in range(heads)]
    after_scores()
    for h in range(heads):
        q_in, k_out, decay = carried[h]
        v = load_v(h)
        st = st_scr[h]
        o = _dot(scores[h], v.astype(BF16)) + _dot_nt(q_in, st.astype(BF16))
        st_scr[h] = st * decay + _dot(v.T.astype(BF16), k_out)
        between(h, o)


def _head_norm_gate(o, gain, gate):
    o = o * lax.rsqrt(jnp.mean(o * o, axis=-1, keepdims=True) + RMS_EPS)
    return o * gain * _silu(gate)


def _log_sigmoid(z):
    return -(jnp.maximum(-z, 0.0) + jnp.log1p(jnp.exp(-jnp.abs(z))))


def _hgrn_lower_bound(logits, j):
    m = jnp.max(logits, axis=0, keepdims=True)
    e = jnp.exp(logits - m)
    sm = e / jnp.sum(e, axis=0, keepdims=True)
    lb = jnp.zeros_like(sm[0:1])
    for i in range(1, j + 1):
        lb = lb + sm[i:i + 1]
    return lb


def _project_and_attend(x_ref, win_ref, proj_scr, mix_scr, mk_ref, mv_ref, main_width):
    xb = x_ref[...].astype(BF16)
    n_in = proj_scr.shape[-1]
    xq_off = n_in - MEM_WIDTH
    proj_scr[:, pl.ds(main_width, n_in - main_width)] = _dot(xb, win_ref[:, pl.ds(main_width, n_in - main_width)])
    starts = list(range(0, main_width, PROJ_PIECE))
    heads_after = {(i * len(starts)) // MEM_HEADS: i for i in range(MEM_HEADS)}
    assert len(heads_after) == MEM_HEADS
    for n, c0 in enumerate(starts):
        cols = pl.ds(c0, min(PROJ_PIECE, main_width - c0))
        proj_scr[:, cols] = _dot(xb, win_ref[:, cols])
        if n in heads_after:
            h = heads_after[n]
            hc = pl.ds(h * MEM_HEAD_DIM, MEM_HEAD_DIM)
            qh = proj_scr[:, pl.ds(xq_off + h * MEM_HEAD_DIM, MEM_HEAD_DIM)].astype(BF16)
            s = _dot_nt(qh, mk_ref[:, hc].astype(BF16)) * MEM_HEAD_DIM ** -0.5
            p = jnp.exp(s - jnp.max(s, axis=-1, keepdims=True))
            p = p / jnp.sum(p, axis=-1, keepdims=True)
            xo = _dot(p.astype(BF16), mv_ref[:, hc].astype(BF16))
            mix_scr[:, pl.ds(D_MODEL + h * MEM_HEAD_DIM, MEM_HEAD_DIM)] = xo.astype(BF16)


def _output(x_ref, mix_scr, wout_ref, lng_ref, lnb_ref, o_ref):
    y = _dot(mix_scr[...], wout_ref[...])
    o_ref[...] = _layer_norm(ALPHA * x_ref[...] + y, lng_ref[...], lnb_ref[...])


def _chunk_loop(stage, restage, proj_scr, mix_scr, st_scr, ng_ref, consts, n_chunks, *, heads, kd, vd, v_off,
                gate_off, restage_per_head):
    first = pl.ds(0, CHUNK)
    for h in range(heads):
        restage(h, first)

    def chunk(c, carry):
        rows = pl.ds(pl.multiple_of(c * CHUNK, CHUNK), CHUNK)
        nxt = pl.ds(pl.multiple_of(jnp.minimum(c + 1, n_chunks - 1) * CHUNK, CHUNK), CHUNK)
        load_v = lambda h: proj_scr[rows, pl.ds(v_off + h * vd, vd)]

        def after_scores():
            if not restage_per_head:
                for h in range(heads):
                    restage(h, nxt)

        def between(h, o):
            if restage_per_head:
                restage(h, nxt)
            gate = proj_scr[rows, pl.ds(gate_off + h * vd, vd)]
            mix_scr[rows, pl.ds(h * vd, vd)] = _head_norm_gate(
                o, ng_ref[:, h * vd:(h + 1) * vd], gate).astype(BF16)

        _recurrence_chunk(stage, load_v, st_scr, consts, heads, kd, after_scores, between)
        return carry

    lax.fori_loop(0, n_chunks, chunk, 0)


def _load_state(s0_ref, st_scr, heads):
    for h in range(heads):
        st_scr[h] = s0_ref[h].T


def _store_state(st_scr, sout_ref, heads):
    for h in range(heads):
        sout_ref[h] = st_scr[h].T


def _hgrn_kernel(x_ref, win_ref, wout_ref, mk_ref, mv_ref, s0_ref, lbl_ref, ng_ref, lng_ref, lnb_ref,
                 o_ref, sout_ref, proj_scr, mix_scr, st_scr, stage_f, stage_b, stage_d, *, layer_j, n_chunks):
    j = pl.program_id(1)

    @pl.when(j == 0)
    def _():
        _load_state(s0_ref, st_scr, HGRN_HEADS)

    _project_and_attend(x_ref, win_ref, proj_scr, mix_scr, mk_ref, mv_ref, HGRN_XQ)
    key_scale = 1.0 - _hgrn_lower_bound(lbl_ref[...], layer_j)
    consts = _chunk_consts(HGRN_HEAD_DIM)
    hd = HGRN_HEAD_DIM
    stage = (stage_f, stage_b, stage_d)

    def restage(h, rows):
        q = _silu(proj_scr[rows, pl.ds(HGRN_Q + h * hd, hd)]) * hd ** -0.5
        k = key_scale[:, h * hd:(h + 1) * hd] * jax.nn.sigmoid(-proj_scr[rows, pl.ds(HGRN_F + h * hd, hd)])
        _stage_head(stage, h, q, k, jnp.log(1.0 - jnp.minimum(k, GATE_CLAMP)), consts["tri"])

    _chunk_loop(stage, restage, proj_scr, mix_scr, st_scr, ng_ref, consts, n_chunks,
                heads=HGRN_HEADS, kd=hd, vd=hd, v_off=HGRN_I, gate_off=HGRN_G, restage_per_head=True)
    _output(x_ref, mix_scr, wout_ref, lng_ref, lnb_ref, o_ref)

    @pl.when(j == pl.num_programs(1) - 1)
    def _():
        _store_state(st_scr, sout_ref, HGRN_HEADS)


def _gla_kernel(x_ref, win_ref, wout_ref, mk_ref, mv_ref, s0_ref, w2_ref, bg_ref, ng_ref, lng_ref, lnb_ref,
                o_ref, sout_ref, proj_scr, mix_scr, st_scr, stage_f, stage_b, stage_d, *, n_chunks):
    j = pl.program_id(1)

    @pl.when(j == 0)
    def _():
        _load_state(s0_ref, st_scr, GLA_HEADS)

    _project_and_attend(x_ref, win_ref, proj_scr, mix_scr, mk_ref, mv_ref, GLA_GA)
    consts = _chunk_consts(GLA_KEY_DIM)
    kd, vd = GLA_KEY_DIM, GLA_VAL_DIM
    stage = (stage_f, stage_b, stage_d)

    def restage(h, rows):
        cs = pl.ds(h * kd, kd)
        ga = proj_scr[rows, pl.ds(GLA_GA, GLA_RANK_PAD)].astype(BF16)
        g = _log_sigmoid(_dot(ga, w2_ref[:, cs]) + bg_ref[:, cs]) / GLA_TAU
        q = proj_scr[rows, pl.ds(GLA_Q + h * kd, kd)] * kd ** -0.5
        _stage_head(stage, h, q, proj_scr[rows, pl.ds(GLA_K + h * kd, kd)], g, consts["tri"])

    _chunk_loop(stage, restage, proj_scr, mix_scr, st_scr, ng_ref, consts, n_chunks,
                heads=GLA_HEADS, kd=kd, vd=vd, v_off=GLA_VO, gate_off=GLA_R, restage_per_head=False)
    _output(x_ref, mix_scr, wout_ref, lng_ref, lnb_ref, o_ref)

    @pl.when(j == pl.num_programs(1) - 1)
    def _():
        _store_state(st_scr, sout_ref, GLA_HEADS)


def _mixer_call(kind, x, w_in, w_out, mem_k, mem_v, s0, extras, ln_g, ln_b, layer, layer_j):
    b, l, d = x.shape
    tt = min(SEQ_BLOCK, l)
    assert l % tt == 0 and tt % CHUNK == 0
    n_in = w_in.shape[-1]
    heads, kd, vd = s0.shape[2:]
    xspec = pl.BlockSpec((None, tt, d), lambda bi, j: (bi, j, 0))
    mspec = pl.BlockSpec((None, None, MEM_TOKENS, MEM_WIDTH), lambda bi, j: (layer, bi, 0, 0))
    s0spec = pl.BlockSpec((None, None, heads, kd, vd), lambda bi, j: (layer_j, bi, 0, 0, 0))
    sspec = pl.BlockSpec((None, heads, kd, vd), lambda bi, j: (bi, 0, 0, 0))
    lnspec = _resident((None, 1, d), lambda bi, j: (layer * 3 + 1, 0, 0))
    if kind == "hgrn":
        body = functools.partial(_hgrn_kernel, layer_j=layer_j, n_chunks=tt // CHUNK)
    else:
        body = functools.partial(_gla_kernel, n_chunks=tt // CHUNK)
    extra_arrays = [a for a, _ in extras]
    extra_specs = [s for _, s in extras]
    return pl.pallas_call(
        body,
        grid=(b, l // tt),
        in_specs=[
            xspec,
            _resident((None, d, n_in), lambda bi, j: (layer_j, 0, 0)),
            _resident((None, MIX_WIDTH, d), lambda bi, j: (layer_j, 0, 0)),
            mspec, mspec, s0spec,
            *extra_specs,
            lnspec, lnspec,
        ],
        out_specs=[xspec, sspec],
        out_shape=[jax.ShapeDtypeStruct((b, l, d), F32), jax.ShapeDtypeStruct(s0.shape[1:], F32)],
        scratch_shapes=[
            pltpu.VMEM((tt, n_in), F32),
            pltpu.VMEM((tt, MIX_WIDTH), BF16),
            pltpu.VMEM((heads, vd, kd), F32),
            pltpu.VMEM((4, CHUNK, heads * kd), F32),
            pltpu.VMEM((2, CHUNK, heads * kd), BF16),
            pltpu.VMEM((1, heads * kd), F32),
        ],
        compiler_params=pltpu.CompilerParams(
            dimension_semantics=("parallel", "arbitrary"), vmem_limit_bytes=VMEM_LIMIT_BYTES),
        name=kind + "_mixer_sublayer",
    )(x, w_in, w_out, mem_k, mem_v, s0, *extra_arrays, ln_g, ln_b)


def _run_trunk(x, mem_k, mem_v, s_hgrn, s_gla, w):
    b, l, d = x.shape
    new_h, new_g = [], []
    for layer in range(DEPTH):
        xf = _ffn_call(x.reshape(b * l, d), w["ffn_g"], w["ffn_u"], w["ffn_d"], w["ln_g"], w["ln_b"], layer, 0, 0)
        x = xf.reshape(b, l, d)
        jl = layer // N_MIXERS
        row = lambda bi, j: (jl, 0, 0)
        if layer % N_MIXERS == 0:
            extras = [
                (w["hgrn_lb"], _resident(w["hgrn_lb"].shape, lambda bi, j: (0, 0))),
                (w["hgrn_norm"], _resident((None, 1, HGRN_WIDTH), row)),
            ]
            x, s = _mixer_call("hgrn", x, w["hgrn_in"], w["hgrn_out"], mem_k, mem_v, s_hgrn,
                               extras, w["ln_g"], w["ln_b"], layer, jl)
            new_h.append(s)
        else:
            extras = [
                (w["gla_w2"], _resident((None, GLA_RANK_PAD, GLA_QK), row)),
                (w["gla_bg"], _resident((None, 1, GLA_QK), row)),
                (w["gla_norm"], _resident((None, 1, GLA_V), row)),
            ]
            x, s = _mixer_call("gla", x, w["gla_in"], w["gla_out"], mem_k, mem_v, s_gla,
                               extras, w["ln_g"], w["ln_b"], layer, jl)
            new_g.append(s)
        xf = _ffn_call(x.reshape(b * l, d), w["ffn_g"], w["ffn_u"], w["ffn_d"], w["ln_g"], w["ln_b"], layer, 1, 2)
        x = xf.reshape(b, l, d)
    return x, jnp.stack(new_h), jnp.stack(new_g)


def _prepare_weights(ffn_w_gate, ffn_w_up, ffn_w_down, ln_gain, ln_bias, hgrn_w_in, hgrn_lb_logits, hgrn_norm,
                     hgrn_w_out, gla_w_in, gla_w_gate2, gla_b_gate, gla_norm, gla_w_out):
    n_gla = gla_w_in.shape[0]
    pad_cols = jnp.zeros((n_gla, D_MODEL, GLA_RANK_PAD - GLA_GATE_RANK), gla_w_in.dtype)
    ga_end = 2 * GLA_QK + 2 * GLA_V + GLA_GATE_RANK
    gla_in = jnp.concatenate([gla_w_in[:, :, :ga_end], pad_cols, gla_w_in[:, :, ga_end:]], axis=-1)
    pad_rows = jnp.zeros((n_gla, GLA_RANK_PAD - GLA_GATE_RANK, GLA_QK), gla_w_gate2.dtype)
    gla_w2 = jnp.concatenate([gla_w_gate2, pad_rows], axis=1)
    return {
        "ffn_g": ffn_w_gate.astype(BF16), "ffn_u": ffn_w_up.astype(BF16), "ffn_d": ffn_w_down.astype(BF16),
        "ln_g": ln_gain.reshape(DEPTH * 3, 1, D_MODEL), "ln_b": ln_bias.reshape(DEPTH * 3, 1, D_MODEL),
        "hgrn_in": hgrn_w_in.astype(BF16), "hgrn_lb": hgrn_lb_logits,
        "hgrn_norm": hgrn_norm.reshape(-1, 1, HGRN_WIDTH), "hgrn_out": hgrn_w_out.astype(BF16),
        "gla_in": gla_in.astype(BF16), "gla_w2": gla_w2.astype(BF16),
        "gla_bg": gla_b_gate.reshape(-1, 1, GLA_QK), "gla_norm": gla_norm.reshape(-1, 1, GLA_V),
        "gla_out": gla_w_out.astype(BF16),
    }


def kernel(x_prompt, x_sample, mem_prompt, cache_mem_k, cache_mem_v, state_hgrn, state_gla,
           ffn_w_gate, ffn_w_up, ffn_w_down, ln_gain, ln_bias,
           hgrn_w_in, hgrn_lb_logits, hgrn_norm, hgrn_w_out,
           gla_w_in, gla_w_gate2, gla_b_gate, gla_norm, gla_w_out, mem_w_k, mem_w_v):
    w = _prepare_weights(ffn_w_gate, ffn_w_up, ffn_w_down, ln_gain, ln_bias, hgrn_w_in, hgrn_lb_logits,
                         hgrn_norm, hgrn_w_out, gla_w_in, gla_w_gate2, gla_b_gate, gla_norm, gla_w_out)
    b = x_prompt.shape[0]
    db = x_sample.shape[0]
    mem_k_flat, mem_v_flat = _memkv_call(mem_prompt.reshape(b * MEM_TOKENS, D_MODEL),
                                         mem_w_k.astype(BF16), mem_w_v.astype(BF16))
    mem_k_p = mem_k_flat.reshape(DEPTH, b, MEM_TOKENS, MEM_WIDTH)
    mem_v_p = mem_v_flat.reshape(DEPTH, b, MEM_TOKENS, MEM_WIDTH)
    n_h, n_g = state_hgrn.shape[0], state_gla.shape[0]
    zeros_h = jnp.zeros((n_h, b) + state_hgrn.shape[2:], F32)
    zeros_g = jnp.zeros((n_g, b) + state_gla.shape[2:], F32)
    y_p, sh_p, sg_p = _run_trunk(x_prompt, mem_k_p, mem_v_p, zeros_h, zeros_g, w)
    y_s, sh_s, sg_s = _run_trunk(x_sample,
                                 cache_mem_k.reshape(DEPTH, db, MEM_TOKENS, MEM_WIDTH),
                                 cache_mem_v.reshape(DEPTH, db, MEM_TOKENS, MEM_WIDTH),
                                 state_hgrn, state_gla, w)
    kv_shape = (DEPTH, b, MEM_TOKENS, MEM_HEADS, MEM_HEAD_DIM)
    return (y_p, y_s, sh_p, sg_p, mem_k_flat.reshape(kv_shape), mem_v_flat.reshape(kv_shape), sh_s, sg_s)
```

```python
import functools

import jax
import jax.numpy as jnp
from jax import lax
from jax.experimental import pallas as pl
from jax.experimental.pallas import tpu as pltpu

F32 = jnp.float32
BF16 = jnp.bfloat16

D_MODEL = 1024
DEPTH = 4
CHUNK = 64
N_MIXERS = 2
HGRN_HEAD_DIM = 128
HGRN_HEADS = D_MODEL // HGRN_HEAD_DIM
HGRN_WIDTH = HGRN_HEADS * HGRN_HEAD_DIM
GLA_HEADS = 4
GLA_KEY_DIM = D_MODEL // (2 * GLA_HEADS)
GLA_VAL_DIM = D_MODEL // GLA_HEADS
GLA_GATE_RANK = 16
GLA_TAU = 16.0
GLA_QK = GLA_HEADS * GLA_KEY_DIM
GLA_V = GLA_HEADS * GLA_VAL_DIM
MEM_TOKENS = 256
MEM_HEADS = 4
MEM_HEAD_DIM = 128
MEM_WIDTH = MEM_HEADS * MEM_HEAD_DIM
D_FF = 2816
ALPHA = (2.0 * DEPTH) ** 0.25
LN_EPS = 1e-5
RMS_EPS = 1e-6
GATE_CLAMP = 1.0 - 1e-6

LANES = 128
SUBLANES = 8
LOG2E = 1.4426950408889634
GLA_RANK_PAD = LANES
HGRN_Q, HGRN_F, HGRN_I, HGRN_G, HGRN_XQ = (i * HGRN_WIDTH for i in range(5))
HGRN_IN = 4 * HGRN_WIDTH + MEM_WIDTH
GLA_Q, GLA_K, GLA_VO = 0, GLA_QK, 2 * GLA_QK
GLA_R = GLA_VO + GLA_V
GLA_GA = GLA_R + GLA_V
GLA_XQ = GLA_GA + GLA_RANK_PAD
GLA_IN_PAD = GLA_XQ + MEM_WIDTH
MIX_WIDTH = D_MODEL + MEM_WIDTH

VMEM_LIMIT_BYTES = 56 * 1024 * 1024
MXU_TILE = 256
TOKEN_BLOCK = 512
FFN_LN_PIECES = 8
SEQ_BLOCK = 512
PROJ_PIECE = 512

_LEVEL_HALVES = (32, 16, 8, 4, 2, 1)


def _dot(a, b):
    return jnp.dot(a, b, preferred_element_type=F32)


def _dot_nt(a, b):
    return lax.dot_general(a, b, (((1,), (1,)), ((), ())), preferred_element_type=F32)


def _layer_norm(z, gain, bias):
    mu = jnp.mean(z, axis=-1, keepdims=True)
    zc = z - mu
    var = jnp.mean(zc * zc, axis=-1, keepdims=True)
    return zc * lax.rsqrt(var + LN_EPS) * gain + bias


def _silu(x):
    return x * jax.nn.sigmoid(x)


def _resident(block_shape, index_map):
    return pl.BlockSpec(block_shape, index_map, pipeline_mode=pl.Buffered(1))


def _ffn_kernel(x_ref, xr_ref, wg_ref, wu_ref, wd_ref, gain_ref, bias_ref, o_ref, y_scr, h_scr):
    i = pl.program_id(0)
    last = pl.num_programs(0) - 1
    rows = x_ref.shape[0]
    col_pieces = [pl.ds(c0, min(MXU_TILE, D_FF - c0)) for c0 in range(0, D_FF, MXU_TILE)]
    ln_rows = rows // FFN_LN_PIECES
    row_pieces = [pl.ds(r0, ln_rows) for r0 in range(0, rows, ln_rows)]

    def finish_previous(rs):
        z = ALPHA * xr_ref[rs, :] + 0.5 * y_scr[rs, :]
        o_ref[rs, :] = _layer_norm(z, gain_ref[...], bias_ref[...])

    def block(with_previous):
        xb = x_ref[...].astype(BF16)
        for p, cols in enumerate(col_pieces):
            g = _dot(xb, wg_ref[:, cols])
            u = _dot(xb, wu_ref[:, cols])
            h_scr[:, cols] = (_silu(g) * u).astype(BF16)
            if with_previous and p < len(row_pieces):
                finish_previous(row_pieces[p])
        y_scr[...] = _dot(h_scr[...], wd_ref[...])

    pl.when(i == 0)(lambda: block(False))
    pl.when((i > 0) & (i < last))(lambda: block(True))

    @pl.when(i == last)
    def _():
        for rs in row_pieces:
            finish_previous(rs)


def _ffn_call(x, wg, wu, wd, ln_g, ln_b, layer, which, ln_idx):
    t, d = x.shape
    tt = min(TOKEN_BLOCK, t)
    assert t % tt == 0
    n = t // tt
    widx = lambda i: (layer, which, 0, 0)
    lidx = lambda i: (layer * 3 + ln_idx, 0, 0)
    prev = pl.BlockSpec((tt, d), lambda i: (jnp.maximum(i - 1, 0), 0))
    return pl.pallas_call(
        _ffn_kernel,
        grid=(n + 1,),
        in_specs=[
            pl.BlockSpec((tt, d), lambda i: (jnp.minimum(i, n - 1), 0)),
            prev,
            _resident((None, None, d, D_FF), widx),
            _resident((None, None, d, D_FF), widx),
            _resident((None, None, D_FF, d), widx),
            _resident((None, 1, d), lidx),
            _resident((None, 1, d), lidx),
        ],
        out_specs=prev,
        out_shape=jax.ShapeDtypeStruct((t, d), F32),
        scratch_shapes=[pltpu.VMEM((tt, d), F32), pltpu.VMEM((tt, D_FF), BF16)],
        compiler_params=pltpu.CompilerParams(
            dimension_semantics=("arbitrary",), vmem_limit_bytes=VMEM_LIMIT_BYTES),
        name="swiglu_sublayer",
    )(x, x, wg, wu, wd, ln_g, ln_b)


def _memkv_kernel(m_ref, wk_ref, wv_ref, k_ref, v_ref):
    mb = m_ref[...].astype(BF16)
    k_ref[...] = _dot(mb, wk_ref[...])
    v_ref[...] = _dot(mb, wv_ref[...])


def _memkv_call(mem, wk, wv):
    t, d = mem.shape
    wspec = pl.BlockSpec((None, d, MEM_WIDTH), lambda l: (l, 0, 0))
    ospec = pl.BlockSpec((None, t, MEM_WIDTH), lambda l: (l, 0, 0))
    oshape = jax.ShapeDtypeStruct((DEPTH, t, MEM_WIDTH), F32)
    return pl.pallas_call(
        _memkv_kernel,
        grid=(DEPTH,),
        in_specs=[_resident((t, d), lambda l: (0, 0)), wspec, wspec],
        out_specs=[ospec, ospec],
        out_shape=[oshape, oshape],
        compiler_params=pltpu.CompilerParams(
            dimension_semantics=("parallel",), vmem_limit_bytes=VMEM_LIMIT_BYTES),
        name="memory_kv_projection",
    )(mem, wk, wv)


def _chunk_consts(kd):
    t = lax.broadcasted_iota(jnp.int32, (CHUNK, CHUNK), 0)
    s = lax.broadcasted_iota(jnp.int32, (CHUNK, CHUNK), 1)
    level_masks = []
    for h in _LEVEL_HALVES:
        lg = h.bit_length() - 1
        ts, ss = t >> lg, s >> lg
        level_masks.append((ts == ss + 1) & ((ss & 1) == 0))
    sub = lax.broadcasted_iota(jnp.int32, (1, SUBLANES, kd), 1)
    return {"levels": level_masks, "eye": t == s, "tri": (t >= s).astype(BF16), "sub": sub}


def _cumsum_rows(tri, g):
    g1 = g.astype(BF16)
    g2 = (g - g1.astype(F32)).astype(BF16)
    return _dot(tri, g1) + _dot(tri, g2)


def _neg_abs(x):
    bits = lax.bitcast_convert_type(x, jnp.int32) | jnp.int32(-2 ** 31)
    return lax.bitcast_convert_type(bits, F32)


def _level_operand(q, k, b2, g2, h, sub):
    kd = q.shape[-1]
    parent = 2 * h
    if parent >= 2 * SUBLANES:
        n = CHUNK // parent
        b4 = b2.reshape(n, parent, kd)
        x = b4 - b4[:, h - 1:h, :]
        src = jnp.concatenate([k.reshape(n, parent, kd)[:, :h, :], q.reshape(n, parent, kd)[:, h:, :]], axis=1)
        return (src * jnp.exp2(_neg_abs(x))).astype(BF16).reshape(CHUNK, kd)
    n = CHUNK // SUBLANES
    b3, q3, k3 = (a.reshape(n, SUBLANES, kd) for a in (b2, q, k))
    upper = (sub & h) != 0
    if h == 4:
        x = _neg_abs(b3 - b3[:, 3:4, :])
    elif h == 2:
        x = _neg_abs(b3 - jnp.where(sub < 4, b3[:, 1:2, :], b3[:, 5:6, :]))
    else:
        x = jnp.where(upper, g2.reshape(n, SUBLANES, kd), 0.0)
    return (jnp.where(upper, q3, k3) * jnp.exp2(x)).astype(BF16).reshape(CHUNK, kd)


def _intra_chunk_scores(q, k, b2, g2, consts):
    a = jnp.where(consts["eye"], _dot_nt(q.astype(BF16), k.astype(BF16)), 0.0)
    for h, mask in zip(_LEVEL_HALVES, consts["levels"]):
        z = _level_operand(q, k, b2, g2, h, consts["sub"])
        a = jnp.where(mask, _dot_nt(z, z), a)
    return a.astype(BF16)


def _stage_head(stage, h, q, k, g, tri):
    stage_f, stage_b, stage_d = stage
    kd = q.shape[-1]
    cs = pl.ds(h * kd, kd)
    b2 = _cumsum_rows(tri, g) * LOG2E
    b2_last = b2[CHUNK - 1:CHUNK, :]
    stage_f[0, :, cs] = q
    stage_f[1, :, cs] = k
    stage_f[2, :, cs] = b2
    stage_f[3, :, cs] = g * LOG2E
    stage_b[0, :, cs] = (q * jnp.exp2(b2)).astype(BF16)
    stage_b[1, :, cs] = (k * jnp.exp2(b2_last - b2)).astype(BF16)
    stage_d[:, cs] = jnp.exp2(b2_last)


def _recurrence_chunk(stage, load_v, st_scr, consts, heads, kd, after_scores, between):
    stage_f, stage_b, stage_d = stage
    scores = []
    for h in range(heads):
        cs = pl.ds(h * kd, kd)
        scores.append(_intra_chunk_scores(stage_f[0, :, cs], stage_f[1, :, cs], stage_f[2, :, cs],
                                          stage_f[3, :, cs], consts))
    carried = [(stage_b[0, :, pl.ds(h * kd, kd)], stage_b[1, :, pl.ds(h * kd, kd)], stage_d[:, pl.ds(h * kd, kd)])
               for h in range(heads)]
    after_scores()
    for h in range(heads):
        q_in, k_out, decay = carried[h]
        v = load_v(h)
        st = st_scr[h]
        o = _dot(scores[h], v.astype(BF16)) + _dot_nt(q_in, st.astype(BF16))
        st_scr[h] = st * decay + _dot(v.T.astype(BF16), k_out)
        between(h, o)


def _head_norm_gate(o, gain, gate):
    o = o * lax.rsqrt(jnp.mean(o * o, axis=-1, keepdims=True) + RMS_EPS)
    return o * gain * _silu(gate)


def _log_sigmoid(z):
    return -(jnp.maximum(-z, 0.0) + jnp.log1p(jnp.exp(-jnp.abs(z))))


def _hgrn_lower_bound(logits, j):
    m = jnp.max(logits, axis=0, keepdims=True)
    e = jnp.exp(logits - m)
    sm = e / jnp.sum(e, axis=0, keepdims=True)
    lb = jnp.zeros_like(sm[0:1])
    for i in range(1, j + 1):
        lb = lb + sm[i:i + 1]
    return lb


def _project_and_attend(x_ref, win_ref, proj_scr, mix_scr, mk_ref, mv_ref, main_width):
    xb = x_ref[...].astype(BF16)
    n_in = proj_scr.shape[-1]
    xq_off = n_in - MEM_WIDTH
    proj_scr[:, pl.ds(main_width, n_in - main_width)] = _dot(xb, win_ref[:, pl.ds(main_width, n_in - main_width)])
    starts = list(range(0, main_width, PROJ_PIECE))
    heads_after = {(i * len(starts)) // MEM_HEADS: i for i in range(MEM_HEADS)}
    assert len(heads_after) == MEM_HEADS
    for n, c0 in enumerate(starts):
        cols = pl.ds(c0, min(PROJ_PIECE, main_width - c0))
        proj_scr[:, cols] = _dot(xb, win_ref[:, cols])
        if n in heads_after:
            h = heads_after[n]
            hc = pl.ds(h * MEM_HEAD_DIM, MEM_HEAD_DIM)
            qh = proj_scr[:, pl.ds(xq_off + h * MEM_HEAD_DIM, MEM_HEAD_DIM)].astype(BF16)
            s = _dot_nt(qh, mk_ref[:, hc].astype(BF16)) * MEM_HEAD_DIM ** -0.5
            p = jnp.exp(s - jnp.max(s, axis=-1, keepdims=True))
            p = p / jnp.sum(p, axis=-1, keepdims=True)
            xo = _dot(p.astype(BF16), mv_ref[:, hc].astype(BF16))
            mix_scr[:, pl.ds(D_MODEL + h * MEM_HEAD_DIM, MEM_HEAD_DIM)] = xo.astype(BF16)


def _output(x_ref, mix_scr, wout_ref, lng_ref, lnb_ref, o_ref):
    rows = x_ref.shape[0]
    half = rows // 2 if rows % (2 * SUBLANES * 2) == 0 else rows
    pieces = [pl.ds(r0, half) for r0 in range(0, rows, half)]
    ys = [_dot(mix_scr[rs, :], wout_ref[...]) for rs in pieces]
    for rs, y in zip(pieces, ys):
        o_ref[rs, :] = _layer_norm(ALPHA * x_ref[rs, :] + y, lng_ref[...], lnb_ref[...])


def _chunk_loop(stage, restage, proj_scr, mix_scr, st_scr, ng_ref, consts, n_chunks, *, heads, kd, vd, v_off,
                gate_off):
    first = pl.ds(0, CHUNK)
    for h in range(heads):
        restage(h, first)

    def chunk(c, carry):
        rows = pl.ds(pl.multiple_of(c * CHUNK, CHUNK), CHUNK)
        nxt = pl.ds(pl.multiple_of(jnp.minimum(c + 1, n_chunks - 1) * CHUNK, CHUNK), CHUNK)
        load_v = lambda h: proj_scr[rows, pl.ds(v_off + h * vd, vd)]

        def after_scores():
            for h in range(heads):
                restage(h, nxt)

        def between(h, o):
            gate = proj_scr[rows, pl.ds(gate_off + h * vd, vd)]
            mix_scr[rows, pl.ds(h * vd, vd)] = _head_norm_gate(
                o, ng_ref[:, h * vd:(h + 1) * vd], gate).astype(BF16)

        _recurrence_chunk(stage, load_v, st_scr, consts, heads, kd, after_scores, between)
        return carry

    lax.fori_loop(0, n_chunks, chunk, 0)


def _load_state(s0_ref, st_scr, heads):
    for h in range(heads):
        st_scr[h] = s0_ref[h].T


def _store_state(st_scr, sout_ref, heads):
    for h in range(heads):
        sout_ref[h] = st_scr[h].T


def _hgrn_kernel(x_ref, win_ref, wout_ref, mk_ref, mv_ref, s0_ref, lbl_ref, ng_ref, lng_ref, lnb_ref,
                 o_ref, sout_ref, proj_scr, mix_scr, st_scr, stage_f, stage_b, stage_d, *, layer_j, n_chunks):
    j = pl.program_id(1)

    @pl.when(j == 0)
    def _():
        _load_state(s0_ref, st_scr, HGRN_HEADS)

    _project_and_attend(x_ref, win_ref, proj_scr, mix_scr, mk_ref, mv_ref, HGRN_XQ)
    key_scale = 1.0 - _hgrn_lower_bound(lbl_ref[...], layer_j)
    consts = _chunk_consts(HGRN_HEAD_DIM)
    hd = HGRN_HEAD_DIM
    stage = (stage_f, stage_b, stage_d)

    def restage(h, rows):
        q = _silu(proj_scr[rows, pl.ds(HGRN_Q + h * hd, hd)]) * hd ** -0.5
        k = key_scale[:, h * hd:(h + 1) * hd] * jax.nn.sigmoid(-proj_scr[rows, pl.ds(HGRN_F + h * hd, hd)])
        _stage_head(stage, h, q, k, jnp.log(1.0 - jnp.minimum(k, GATE_CLAMP)), consts["tri"])

    _chunk_loop(stage, restage, proj_scr, mix_scr, st_scr, ng_ref, consts, n_chunks,
                heads=HGRN_HEADS, kd=hd, vd=hd, v_off=HGRN_I, gate_off=HGRN_G)
    _output(x_ref, mix_scr, wout_ref, lng_ref, lnb_ref, o_ref)

    @pl.when(j == pl.num_programs(1) - 1)
    def _():
        _store_state(st_scr, sout_ref, HGRN_HEADS)


def _gla_kernel(x_ref, win_ref, wout_ref, mk_ref, mv_ref, s0_ref, w2_ref, bg_ref, ng_ref, lng_ref, lnb_ref,
                o_ref, sout_ref, proj_scr, mix_scr, st_scr, stage_f, stage_b, stage_d, *, n_chunks):
    j = pl.program_id(1)

    @pl.when(j == 0)
    def _():
        _load_state(s0_ref, st_scr, GLA_HEADS)

    _project_and_attend(x_ref, win_ref, proj_scr, mix_scr, mk_ref, mv_ref, GLA_GA)
    consts = _chunk_consts(GLA_KEY_DIM)
    kd, vd = GLA_KEY_DIM, GLA_VAL_DIM
    stage = (stage_f, stage_b, stage_d)

    def restage(h, rows):
        cs = pl.ds(h * kd, kd)
        ga = proj_scr[rows, pl.ds(GLA_GA, GLA_RANK_PAD)].astype(BF16)
        g = _log_sigmoid(_dot(ga, w2_ref[:, cs]) + bg_ref[:, cs]) / GLA_TAU
        q = proj_scr[rows, pl.ds(GLA_Q + h * kd, kd)] * kd ** -0.5
        _stage_head(stage, h, q, proj_scr[rows, pl.ds(GLA_K + h * kd, kd)], g, consts["tri"])

    _chunk_loop(stage, restage, proj_scr, mix_scr, st_scr, ng_ref, consts, n_chunks,
                heads=GLA_HEADS, kd=kd, vd=vd, v_off=GLA_VO, gate_off=GLA_R)
    _output(x_ref, mix_scr, wout_ref, lng_ref, lnb_ref, o_ref)

    @pl.when(j == pl.num_programs(1) - 1)
    def _():
        _store_state(st_scr, sout_ref, GLA_HEADS)


def _mixer_call(kind, x, w_in, w_out, mem_k, mem_v, s0, extras, ln_g, ln_b, layer, layer_j):
    b, l, d = x.shape
    tt = min(SEQ_BLOCK, l)
    assert l % tt == 0 and tt % CHUNK == 0
    n_in = w_in.shape[-1]
    heads, kd, vd = s0.shape[2:]
    xspec = pl.BlockSpec((None, tt, d), lambda bi, j: (bi, j, 0))
    mspec = pl.BlockSpec((None, None, MEM_TOKENS, MEM_WIDTH), lambda bi, j: (layer, bi, 0, 0))
    s0spec = pl.BlockSpec((None, None, heads, kd, vd), lambda bi, j: (layer_j, bi, 0, 0, 0))
    sspec = pl.BlockSpec((None, heads, kd, vd), lambda bi, j: (bi, 0, 0, 0))
    lnspec = _resident((None, 1, d), lambda bi, j: (layer * 3 + 1, 0, 0))
    if kind == "hgrn":
        body = functools.partial(_hgrn_kernel, layer_j=layer_j, n_chunks=tt // CHUNK)
    else:
        body = functools.partial(_gla_kernel, n_chunks=tt // CHUNK)
    extra_arrays = [a for a, _ in extras]
    extra_specs = [s for _, s in extras]
    return pl.pallas_call(
        body,
        grid=(b, l // tt),
        in_specs=[
            xspec,
            _resident((None, d, n_in), lambda bi, j: (layer_j, 0, 0)),
            _resident((None, MIX_WIDTH, d), lambda bi, j: (layer_j, 0, 0)),
            mspec, mspec, s0spec,
            *extra_specs,
            lnspec, lnspec,
        ],
        out_specs=[xspec, sspec],
        out_shape=[jax.ShapeDtypeStruct((b, l, d), F32), jax.ShapeDtypeStruct(s0.shape[1:], F32)],
        scratch_shapes=[
            pltpu.VMEM((tt, n_in), F32),
            pltpu.VMEM((tt, MIX_WIDTH), BF16),
            pltpu.VMEM((heads, vd, kd), F32),
            pltpu.VMEM((4, CHUNK, heads * kd), F32),
            pltpu.VMEM((2, CHUNK, heads * kd), BF16),
            pltpu.VMEM((1, heads * kd), F32),
        ],
        compiler_params=pltpu.CompilerParams(
            dimension_semantics=("parallel", "arbitrary"), vmem_limit_bytes=VMEM_LIMIT_BYTES),
        name=kind + "_mixer_sublayer",
    )(x, w_in, w_out, mem_k, mem_v, s0, *extra_arrays, ln_g, ln_b)


def _run_trunk(x, mem_k, mem_v, s_hgrn, s_gla, w):
    b, l, d = x.shape
    new_h, new_g = [], []
    for layer in range(DEPTH):
        xf = _ffn_call(x.reshape(b * l, d), w["ffn_g"], w["ffn_u"], w["ffn_d"], w["ln_g"], w["ln_b"], layer, 0, 0)
        x = xf.reshape(b, l, d)
        jl = layer // N_MIXERS
        row = lambda bi, j: (jl, 0, 0)
        if layer % N_MIXERS == 0:
            extras = [
                (w["hgrn_lb"], _resident(w["hgrn_lb"].shape, lambda bi, j: (0, 0))),
                (w["hgrn_norm"], _resident((None, 1, HGRN_WIDTH), row)),
            ]
            x, s = _mixer_call("hgrn", x, w["hgrn_in"], w["hgrn_out"], mem_k, mem_v, s_hgrn,
                               extras, w["ln_g"], w["ln_b"], layer, jl)
            new_h.append(s)
        else:
            extras = [
                (w["gla_w2"], _resident((None, GLA_RANK_PAD, GLA_QK), row)),
                (w["gla_bg"], _resident((None, 1, GLA_QK), row)),
                (w["gla_norm"], _resident((None, 1, GLA_V), row)),
            ]
            x, s = _mixer_call("gla", x, w["gla_in"], w["gla_out"], mem_k, mem_v, s_gla,
                               extras, w["ln_g"], w["ln_b"], layer, jl)
            new_g.append(s)
        xf = _ffn_call(x.reshape(b * l, d), w["ffn_g"], w["ffn_u"], w["ffn_d"], w["ln_g"], w["ln_b"], layer, 1, 2)
        x = xf.reshape(b, l, d)
    return x, jnp.stack(new_h), jnp.stack(new_g)


def _prepare_weights(ffn_w_gate, ffn_w_up, ffn_w_down, ln_gain, ln_bias, hgrn_w_in, hgrn_lb_logits, hgrn_norm,
                     hgrn_w_out, gla_w_in, gla_w_gate2, gla_b_gate, gla_norm, gla_w_out):
    n_gla = gla_w_in.shape[0]
    pad_cols = jnp.zeros((n_gla, D_MODEL, GLA_RANK_PAD - GLA_GATE_RANK), gla_w_in.dtype)
    ga_end = 2 * GLA_QK + 2 * GLA_V + GLA_GATE_RANK
    gla_in = jnp.concatenate([gla_w_in[:, :, :ga_end], pad_cols, gla_w_in[:, :, ga_end:]], axis=-1)
    pad_rows = jnp.zeros((n_gla, GLA_RANK_PAD - GLA_GATE_RANK, GLA_QK), gla_w_gate2.dtype)
    gla_w2 = jnp.concatenate([gla_w_gate2, pad_rows], axis=1)
    return {
        "ffn_g": ffn_w_gate.astype(BF16), "ffn_u": ffn_w_up.astype(BF16), "ffn_d": ffn_w_down.astype(BF16),
        "ln_g": ln_gain.reshape(DEPTH * 3, 1, D_MODEL), "ln_b": ln_bias.reshape(DEPTH * 3, 1, D_MODEL),
        "hgrn_in": hgrn_w_in.astype(BF16), "hgrn_lb": hgrn_lb_logits,
        "hgrn_norm": hgrn_norm.reshape(-1, 1, HGRN_WIDTH), "hgrn_out": hgrn_w_out.astype(BF16),
        "gla_in": gla_in.astype(BF16), "gla_w2": gla_w2.astype(BF16),
        "gla_bg": gla_b_gate.reshape(-1, 1, GLA_QK), "gla_norm": gla_norm.reshape(-1, 1, GLA_V),
        "gla_out": gla_w_out.astype(BF16),
    }


def kernel(x_prompt, x_sample, mem_prompt, cache_mem_k, cache_mem_v, state_hgrn, state_gla,
           ffn_w_gate, ffn_w_up, ffn_w_down, ln_gain, ln_bias,
           hgrn_w_in, hgrn_lb_logits, hgrn_norm, hgrn_w_out,
           gla_w_in, gla_w_gate2, gla_b_gate, gla_norm, gla_w_out, mem_w_k, mem_w_v):
    w = _prepare_weights(ffn_w_gate, ffn_w_up, ffn_w_down, ln_gain, ln_bias, hgrn_w_in, hgrn_lb_logits,
                         hgrn_norm, hgrn_w_out, gla_w_in, gla_w_gate2, gla_b_gate, gla_norm, gla_w_out)
    b = x_prompt.shape[0]
    db = x_sample.shape[0]
    mem_k_flat, mem_v_flat = _memkv_call(mem_prompt.reshape(b * MEM_TOKENS, D_MODEL),
                                         mem_w_k.astype(BF16), mem_w_v.astype(BF16))
    mem_k_p = mem_k_flat.reshape(DEPTH, b, MEM_TOKENS, MEM_WIDTH)
    mem_v_p = mem_v_flat.reshape(DEPTH, b, MEM_TOKENS, MEM_WIDTH)
    n_h, n_g = state_hgrn.shape[0], state_gla.shape[0]
    zeros_h = jnp.zeros((n_h, b) + state_hgrn.shape[2:], F32)
    zeros_g = jnp.zeros((n_g, b) + state_gla.shape[2:], F32)
    y_p, sh_p, sg_p = _run_trunk(x_prompt, mem_k_p, mem_v_p, zeros_h, zeros_g, w)
    y_s, sh_s, sg_s = _run_trunk(x_sample,
                                 cache_mem_k.reshape(DEPTH, db, MEM_TOKENS, MEM_WIDTH),
                                 cache_mem_v.reshape(DEPTH, db, MEM_TOKENS, MEM_WIDTH),
                                 state_hgrn, state_gla, w)
    kv_shape = (DEPTH, b, MEM_TOKENS, MEM_HEADS, MEM_HEAD_DIM)
    return (y_p, y_s, sh_p, sg_p, mem_k_flat.reshape(kv_shape), mem_v_flat.reshape(kv_shape), sh_s, sg_s)
```

```python
import functools

import jax
import jax.numpy as jnp
from jax import lax
from jax.experimental import pallas as pl
from jax.experimental.pallas import tpu as pltpu

F32 = jnp.float32
BF16 = jnp.bfloat16

D_MODEL = 1024
DEPTH = 4
CHUNK = 64
N_MIXERS = 2
HGRN_HEAD_DIM = 128
HGRN_HEADS = D_MODEL // HGRN_HEAD_DIM
HGRN_WIDTH = HGRN_HEADS * HGRN_HEAD_DIM
GLA_HEADS = 4
GLA_KEY_DIM = D_MODEL // (2 * GLA_HEADS)
GLA_VAL_DIM = D_MODEL // GLA_HEADS
GLA_GATE_RANK = 16
GLA_TAU = 16.0
GLA_QK = GLA_HEADS * GLA_KEY_DIM
GLA_V = GLA_HEADS * GLA_VAL_DIM
MEM_TOKENS = 256
MEM_HEADS = 4
MEM_HEAD_DIM = 128
MEM_WIDTH = MEM_HEADS * MEM_HEAD_DIM
D_FF = 2816
ALPHA = (2.0 * DEPTH) ** 0.25
LN_EPS = 1e-5
RMS_EPS = 1e-6
GATE_CLAMP = 1.0 - 1e-6

LANES = 128
SUBLANES = 8
LOG2E = 1.4426950408889634
GLA_RANK_PAD = LANES
HGRN_Q, HGRN_F, HGRN_I, HGRN_G, HGRN_XQ = (i * HGRN_WIDTH for i in range(5))
HGRN_IN = 4 * HGRN_WIDTH + MEM_WIDTH
GLA_Q, GLA_K, GLA_VO = 0, GLA_QK, 2 * GLA_QK
GLA_R = GLA_VO + GLA_V
GLA_GA = GLA_R + GLA_V
GLA_XQ = GLA_GA + GLA_RANK_PAD
GLA_IN_PAD = GLA_XQ + MEM_WIDTH
MIX_WIDTH = D_MODEL + MEM_WIDTH

VMEM_LIMIT_BYTES = 56 * 1024 * 1024
MXU_TILE = 256
TOKEN_BLOCK = 512
FFN_LN_PIECES = 8
SEQ_BLOCK = 512
PROJ_PIECE = 512
_LEVEL_HALVES = (32, 16, 8, 4, 2, 1)


def _dot(a, b):
    return jnp.dot(a, b, preferred_element_type=F32)


def _dot_nt(a, b):
    return lax.dot_general(a, b, (((1,), (1,)), ((), ())), preferred_element_type=F32)


def _layer_norm(z, gain, bias):
    mu = jnp.mean(z, axis=-1, keepdims=True)
    zc = z - mu
    var = jnp.mean(zc * zc, axis=-1, keepdims=True)
    return zc * lax.rsqrt(var + LN_EPS) * gain + bias


def _silu(x):
    return x * jax.nn.sigmoid(x)


def _resident(block_shape, index_map):
    return pl.BlockSpec(block_shape, index_map, pipeline_mode=pl.Buffered(1))


def _ffn_kernel(x_ref, xr_ref, wg_ref, wu_ref, wd_ref, gain_ref, bias_ref, o_ref, y_scr, h_scr):
    i = pl.program_id(0)
    last = pl.num_programs(0) - 1
    rows = x_ref.shape[0]
    col_pieces = [pl.ds(c0, min(MXU_TILE, D_FF - c0)) for c0 in range(0, D_FF, MXU_TILE)]
    ln_rows = rows // FFN_LN_PIECES
    row_pieces = [pl.ds(r0, ln_rows) for r0 in range(0, rows, ln_rows)]

    def finish_previous(rs):
        z = ALPHA * xr_ref[rs, :] + 0.5 * y_scr[rs, :]
        o_ref[rs, :] = _layer_norm(z, gain_ref[...], bias_ref[...])

    def block(with_previous):
        xb = x_ref[...].astype(BF16)
        for p, cols in enumerate(col_pieces):
            g = _dot(xb, wg_ref[:, cols])
            u = _dot(xb, wu_ref[:, cols])
            h_scr[:, cols] = (_silu(g) * u).astype(BF16)
            if with_previous and p < len(row_pieces):
                finish_previous(row_pieces[p])
        y_scr[...] = _dot(h_scr[...], wd_ref[...])

    pl.when(i == 0)(lambda: block(False))
    pl.when((i > 0) & (i < last))(lambda: block(True))

    @pl.when(i == last)
    def _():
        for rs in row_pieces:
            finish_previous(rs)


def _ffn_call(x, wg, wu, wd, ln_g, ln_b, layer, which, ln_idx):
    t, d = x.shape
    tt = min(TOKEN_BLOCK, t)
    assert t % tt == 0
    n = t // tt
    widx = lambda i: (layer, which, 0, 0)
    lidx = lambda i: (layer * 3 + ln_idx, 0, 0)
    prev = pl.BlockSpec((tt, d), lambda i: (jnp.maximum(i - 1, 0), 0))
    return pl.pallas_call(
        _ffn_kernel,
        grid=(n + 1,),
        in_specs=[
            pl.BlockSpec((tt, d), lambda i: (jnp.minimum(i, n - 1), 0)),
            prev,
            _resident((None, None, d, D_FF), widx),
            _resident((None, None, d, D_FF), widx),
            _resident((None, None, D_FF, d), widx),
            _resident((None, 1, d), lidx),
            _resident((None, 1, d), lidx),
        ],
        out_specs=prev,
        out_shape=jax.ShapeDtypeStruct((t, d), F32),
        scratch_shapes=[pltpu.VMEM((tt, d), F32), pltpu.VMEM((tt, D_FF), BF16)],
        compiler_params=pltpu.CompilerParams(
            dimension_semantics=("arbitrary",), vmem_limit_bytes=VMEM_LIMIT_BYTES),
        name="swiglu_sublayer",
    )(x, x, wg, wu, wd, ln_g, ln_b)


def _memkv_kernel(m_ref, wk_ref, wv_ref, k_ref, v_ref):
    mb = m_ref[...].astype(BF16)
    k_ref[...] = _dot(mb, wk_ref[...])
    v_ref[...] = _dot(mb, wv_ref[...])


def _memkv_call(mem, wk, wv):
    t, d = mem.shape
    wspec = pl.BlockSpec((None, d, MEM_WIDTH), lambda l: (l, 0, 0))
    ospec = pl.BlockSpec((None, t, MEM_WIDTH), lambda l: (l, 0, 0))
    oshape = jax.ShapeDtypeStruct((DEPTH, t, MEM_WIDTH), F32)
    return pl.pallas_call(
        _memkv_kernel,
        grid=(DEPTH,),
        in_specs=[_resident((t, d), lambda l: (0, 0)), wspec, wspec],
        out_specs=[ospec, ospec],
        out_shape=[oshape, oshape],
        compiler_params=pltpu.CompilerParams(
            dimension_semantics=("parallel",), vmem_limit_bytes=VMEM_LIMIT_BYTES),
        name="memory_kv_projection",
    )(mem, wk, wv)


def _chunk_consts(kd):
    t = lax.broadcasted_iota(jnp.int32, (CHUNK, CHUNK), 0)
    s = lax.broadcasted_iota(jnp.int32, (CHUNK, CHUNK), 1)
    level_masks = []
    for h in _LEVEL_HALVES:
        lg = h.bit_length() - 1
        ts, ss = t >> lg, s >> lg
        level_masks.append((ts == ss + 1) & ((ss & 1) == 0))
    sub = lax.broadcasted_iota(jnp.int32, (1, SUBLANES, kd), 1)
    return {"levels": level_masks, "eye": t == s, "tri": (t >= s).astype(BF16), "sub": sub}


def _cumsum_rows(tri, g):
    g1 = g.astype(BF16)
    g2 = (g - g1.astype(F32)).astype(BF16)
    return _dot(tri, g1) + _dot(tri, g2)


def _neg_abs(x):
    bits = lax.bitcast_convert_type(x, jnp.int32) | jnp.int32(-2 ** 31)
    return lax.bitcast_convert_type(bits, F32)


def _level_operand(q, k, b2, g2, h, sub):
    kd = q.shape[-1]
    parent = 2 * h
    if parent >= 2 * SUBLANES:
        n = CHUNK // parent
        b4 = b2.reshape(n, parent, kd)
        x = b4 - b4[:, h - 1:h, :]
        src = jnp.concatenate([k.reshape(n, parent, kd)[:, :h, :], q.reshape(n, parent, kd)[:, h:, :]], axis=1)
        return (src * jnp.exp2(_neg_abs(x))).astype(BF16).reshape(CHUNK, kd)
    n = CHUNK // SUBLANES
    b3, q3, k3 = (a.reshape(n, SUBLANES, kd) for a in (b2, q, k))
    upper = (sub & h) != 0
    if h == 4:
        x = _neg_abs(b3 - b3[:, 3:4, :])
    elif h == 2:
        x = _neg_abs(b3 - jnp.where(sub < 4, b3[:, 1:2, :], b3[:, 5:6, :]))
    else:
        x = jnp.where(upper, g2.reshape(n, SUBLANES, kd), 0.0)
    return (jnp.where(upper, q3, k3) * jnp.exp2(x)).astype(BF16).reshape(CHUNK, kd)


def _intra_chunk_scores(q, k, b2, g2, consts):
    a = jnp.where(consts["eye"], _dot_nt(q.astype(BF16), k.astype(BF16)), 0.0)
    for h, mask in zip(_LEVEL_HALVES, consts["levels"]):
        z = _level_operand(q, k, b2, g2, h, consts["sub"])
        a = jnp.where(mask, _dot_nt(z, z), a)
    return a.astype(BF16)


def _stage_head(stage, h, q, k, g, tri):
    stage_f, stage_b, stage_d = stage
    kd = q.shape[-1]
    cs = pl.ds(h * kd, kd)
    b2 = _cumsum_rows(tri, g) * LOG2E
    b2_last = b2[CHUNK - 1:CHUNK, :]
    stage_f[0, :, cs] = q
    stage_f[1, :, cs] = k
    stage_f[2, :, cs] = b2
    stage_f[3, :, cs] = g * LOG2E
    stage_b[0, :, cs] = (q * jnp.exp2(b2)).astype(BF16)
    stage_b[1, :, cs] = (k * jnp.exp2(b2_last - b2)).astype(BF16)
    stage_d[:, cs] = jnp.exp2(b2_last)


def _recurrence_chunk(stage, load_v, st_scr, consts, heads, kd, with_scores, after_scores, emit):
    stage_f, stage_b, stage_d = stage
    scores = []
    for h in range(heads):
        cs = pl.ds(h * kd, kd)
        scores.append(_intra_chunk_scores(stage_f[0, :, cs], stage_f[1, :, cs], stage_f[2, :, cs],
                                          stage_f[3, :, cs], consts))
        with_scores(h)
    carried = [(stage_b[0, :, pl.ds(h * kd, kd)], stage_b[1, :, pl.ds(h * kd, kd)], stage_d[:, pl.ds(h * kd, kd)])
               for h in range(heads)]
    after_scores()
    for h in range(heads):
        q_in, k_out, decay = carried[h]
        v = load_v(h)
        st = st_scr[h]
        o = _dot(scores[h], v.astype(BF16)) + _dot_nt(q_in, st.astype(BF16))
        st_scr[h] = st * decay + _dot(v.T.astype(BF16), k_out)
        emit(h, o)


def _head_norm_gate(o, gain, gate):
    o = o * lax.rsqrt(jnp.mean(o * o, axis=-1, keepdims=True) + RMS_EPS)
    return o * gain * _silu(gate)


def _log_sigmoid(z):
    return -(jnp.maximum(-z, 0.0) + jnp.log1p(jnp.exp(-jnp.abs(z))))


def _hgrn_lower_bound(logits, j):
    m = jnp.max(logits, axis=0, keepdims=True)
    e = jnp.exp(logits - m)
    sm = e / jnp.sum(e, axis=0, keepdims=True)
    lb = jnp.zeros_like(sm[0:1])
    for i in range(1, j + 1):
        lb = lb + sm[i:i + 1]
    return lb


def _project_and_attend(x_ref, win_ref, proj_scr, mix_scr, mk_ref, mv_ref, main_width):
    xb = x_ref[...].astype(BF16)
    n_in = proj_scr.shape[-1]
    xq_off = n_in - MEM_WIDTH
    proj_scr[:, pl.ds(main_width, n_in - main_width)] = _dot(xb, win_ref[:, pl.ds(main_width, n_in - main_width)])
    starts = list(range(0, main_width, PROJ_PIECE))
    heads_after = {(i * len(starts)) // MEM_HEADS: i for i in range(MEM_HEADS)}
    assert len(heads_after) == MEM_HEADS
    for n, c0 in enumerate(starts):
        cols = pl.ds(c0, min(PROJ_PIECE, main_width - c0))
        proj_scr[:, cols] = _dot(xb, win_ref[:, cols])
        if n in heads_after:
            h = heads_after[n]
            hc = pl.ds(h * MEM_HEAD_DIM, MEM_HEAD_DIM)
            qh = proj_scr[:, pl.ds(xq_off + h * MEM_HEAD_DIM, MEM_HEAD_DIM)].astype(BF16)
            s = _dot_nt(qh, mk_ref[:, hc].astype(BF16)) * MEM_HEAD_DIM ** -0.5
            p = jnp.exp(s - jnp.max(s, axis=-1, keepdims=True))
            p = p / jnp.sum(p, axis=-1, keepdims=True)
            xo = _dot(p.astype(BF16), mv_ref[:, hc].astype(BF16))
            mix_scr[:, pl.ds(D_MODEL + h * MEM_HEAD_DIM, MEM_HEAD_DIM)] = xo.astype(BF16)


def _output(x_ref, mix_scr, wout_ref, lng_ref, lnb_ref, o_ref):
    rows = x_ref.shape[0]
    half = rows // 2 if rows % (2 * SUBLANES * 2) == 0 else rows
    pieces = [pl.ds(r0, half) for r0 in range(0, rows, half)]
    ys = [_dot(mix_scr[rs, :], wout_ref[...]) for rs in pieces]
    for rs, y in zip(pieces, ys):
        o_ref[rs, :] = _layer_norm(ALPHA * x_ref[rs, :] + y, lng_ref[...], lnb_ref[...])


def _chunk_loop(stage, restage, proj_scr, mix_scr, st_scr, o_scr, ng_ref, consts, n_chunks, *, heads, kd, vd,
                v_off, gate_off):
    first = pl.ds(0, CHUNK)
    for h in range(heads):
        restage(h, first)
    o_scr[...] = jnp.zeros(o_scr.shape, F32)

    def finish(h, rows):
        hv = pl.ds(h * vd, vd)
        gate = proj_scr[rows, pl.ds(gate_off + h * vd, vd)]
        mix_scr[rows, hv] = _head_norm_gate(o_scr[:, hv], ng_ref[:, h * vd:(h + 1) * vd], gate).astype(BF16)

    def store_output(h, o):
        o_scr[:, pl.ds(h * vd, vd)] = o

    def chunk(c, carry):
        rows = pl.ds(pl.multiple_of(c * CHUNK, CHUNK), CHUNK)
        prv = pl.ds(pl.multiple_of(jnp.maximum(c - 1, 0) * CHUNK, CHUNK), CHUNK)
        nxt = pl.ds(pl.multiple_of(jnp.minimum(c + 1, n_chunks - 1) * CHUNK, CHUNK), CHUNK)
        load_v = lambda h: proj_scr[rows, pl.ds(v_off + h * vd, vd)]

        def after_scores():
            for h in range(heads):
                restage(h, nxt)

        _recurrence_chunk(stage, load_v, st_scr, consts, heads, kd, lambda h: finish(h, prv), after_scores,
                          store_output)
        return carry

    lax.fori_loop(0, n_chunks, chunk, 0)
    for h in range(heads):
        finish(h, pl.ds((n_chunks - 1) * CHUNK, CHUNK))


def _load_state(s0_ref, st_scr, heads):
    for h in range(heads):
        st_scr[h] = s0_ref[h].T


def _store_state(st_scr, sout_ref, heads):
    for h in range(heads):
        sout_ref[h] = st_scr[h].T


def _hgrn_kernel(x_ref, win_ref, wout_ref, mk_ref, mv_ref, s0_ref, lbl_ref, ng_ref, lng_ref, lnb_ref,
                 o_ref, sout_ref, proj_scr, mix_scr, st_scr, stage_f, stage_b, stage_d, o_scr, *, layer_j, n_chunks):
    j = pl.program_id(1)

    @pl.when(j == 0)
    def _():
        _load_state(s0_ref, st_scr, HGRN_HEADS)

    key_scale = 1.0 - _hgrn_lower_bound(lbl_ref[...], layer_j)
    consts = _chunk_consts(HGRN_HEAD_DIM)
    hd = HGRN_HEAD_DIM
    stage = (stage_f, stage_b, stage_d)

    def restage(h, rows):
        q = _silu(proj_scr[rows, pl.ds(HGRN_Q + h * hd, hd)]) * hd ** -0.5
        k = key_scale[:, h * hd:(h + 1) * hd] * jax.nn.sigmoid(-proj_scr[rows, pl.ds(HGRN_F + h * hd, hd)])
        _stage_head(stage, h, q, k, jnp.log(1.0 - jnp.minimum(k, GATE_CLAMP)), consts["tri"])

    _project_and_attend(x_ref, win_ref, proj_scr, mix_scr, mk_ref, mv_ref, HGRN_XQ)
    _chunk_loop(stage, restage, proj_scr, mix_scr, st_scr, o_scr, ng_ref, consts, n_chunks,
                heads=HGRN_HEADS, kd=hd, vd=hd, v_off=HGRN_I, gate_off=HGRN_G)
    _output(x_ref, mix_scr, wout_ref, lng_ref, lnb_ref, o_ref)

    @pl.when(j == pl.num_programs(1) - 1)
    def _():
        _store_state(st_scr, sout_ref, HGRN_HEADS)


def _gla_kernel(x_ref, win_ref, wout_ref, mk_ref, mv_ref, s0_ref, w2_ref, bg_ref, ng_ref, lng_ref, lnb_ref,
                o_ref, sout_ref, proj_scr, mix_scr, st_scr, stage_f, stage_b, stage_d, o_scr, *, n_chunks):
    j = pl.program_id(1)

    @pl.when(j == 0)
    def _():
        _load_state(s0_ref, st_scr, GLA_HEADS)

    consts = _chunk_consts(GLA_KEY_DIM)
    kd, vd = GLA_KEY_DIM, GLA_VAL_DIM
    stage = (stage_f, stage_b, stage_d)

    def restage(h, rows):
        cs = pl.ds(h * kd, kd)
        ga = proj_scr[rows, pl.ds(GLA_GA, GLA_RANK_PAD)].astype(BF16)
        g = _log_sigmoid(_dot(ga, w2_ref[:, cs]) + bg_ref[:, cs]) / GLA_TAU
        q = proj_scr[rows, pl.ds(GLA_Q + h * kd, kd)] * kd ** -0.5
        _stage_head(stage, h, q, proj_scr[rows, pl.ds(GLA_K + h * kd, kd)], g, consts["tri"])

    _project_and_attend(x_ref, win_ref, proj_scr, mix_scr, mk_ref, mv_ref, GLA_GA)

    _chunk_loop(stage, restage, proj_scr, mix_scr, st_scr, o_scr, ng_ref, consts, n_chunks,
                heads=GLA_HEADS, kd=kd, vd=vd, v_off=GLA_VO, gate_off=GLA_R)
    _output(x_ref, mix_scr, wout_ref, lng_ref, lnb_ref, o_ref)

    @pl.when(j == pl.num_programs(1) - 1)
    def _():
        _store_state(st_scr, sout_ref, GLA_HEADS)


def _mixer_call(kind, x, w_in, w_out, mem_k, mem_v, s0, extras, ln_g, ln_b, layer, layer_j):
    b, l, d = x.shape
    tt = min(SEQ_BLOCK, l)
    assert l % tt == 0 and tt % CHUNK == 0
    n_in = w_in.shape[-1]
    heads, kd, vd = s0.shape[2:]
    xspec = pl.BlockSpec((None, tt, d), lambda bi, j: (bi, j, 0))
    mspec = pl.BlockSpec((None, None, MEM_TOKENS, MEM_WIDTH), lambda bi, j: (layer, bi, 0, 0))
    s0spec = pl.BlockSpec((None, None, heads, kd, vd), lambda bi, j: (layer_j, bi, 0, 0, 0))
    sspec = pl.BlockSpec((None, heads, kd, vd), lambda bi, j: (bi, 0, 0, 0))
    lnspec = _resident((None, 1, d), lambda bi, j: (layer * 3 + 1, 0, 0))
    if kind == "hgrn":
        body = functools.partial(_hgrn_kernel, layer_j=layer_j, n_chunks=tt // CHUNK)
    else:
        body = functools.partial(_gla_kernel, n_chunks=tt // CHUNK)
    extra_arrays = [a for a, _ in extras]
    extra_specs = [s for _, s in extras]
    return pl.pallas_call(
        body,
        grid=(b, l // tt),
        in_specs=[
            xspec,
            _resident((None, d, n_in), lambda bi, j: (layer_j, 0, 0)),
            _resident((None, MIX_WIDTH, d), lambda bi, j: (layer_j, 0, 0)),
            mspec, mspec, s0spec,
            *extra_specs,
            lnspec, lnspec,
        ],
        out_specs=[xspec, sspec],
        out_shape=[jax.ShapeDtypeStruct((b, l, d), F32), jax.ShapeDtypeStruct(s0.shape[1:], F32)],
        scratch_shapes=[
            pltpu.VMEM((tt, n_in), F32),
            pltpu.VMEM((tt, MIX_WIDTH), BF16),
            pltpu.VMEM((heads, vd, kd), F32),
            pltpu.VMEM((4, CHUNK, heads * kd), F32),
            pltpu.VMEM((2, CHUNK, heads * kd), BF16),
            pltpu.VMEM((1, heads * kd), F32),
            pltpu.VMEM((CHUNK, heads * vd), F32),
        ],
        compiler_params=pltpu.CompilerParams(
            dimension_semantics=("parallel", "arbitrary"), vmem_limit_bytes=VMEM_LIMIT_BYTES),
        name=kind + "_mixer_sublayer",
    )(x, w_in, w_out, mem_k, mem_v, s0, *extra_arrays, ln_g, ln_b)


def _run_trunk(x, mem_k, mem_v, s_hgrn, s_gla, w):
    b, l, d = x.shape
    new_h, new_g = [], []
    for layer in range(DEPTH):
        xf = _ffn_call(x.reshape(b * l, d), w["ffn_g"], w["ffn_u"], w["ffn_d"], w["ln_g"], w["ln_b"], layer, 0, 0)
        x = xf.reshape(b, l, d)
        jl = layer // N_MIXERS
        row = lambda bi, j: (jl, 0, 0)
        if layer % N_MIXERS == 0:
            extras = [
                (w["hgrn_lb"], _resident(w["hgrn_lb"].shape, lambda bi, j: (0, 0))),
                (w["hgrn_norm"], _resident((None, 1, HGRN_WIDTH), row)),
            ]
            x, s = _mixer_call("hgrn", x, w["hgrn_in"], w["hgrn_out"], mem_k, mem_v, s_hgrn,
                               extras, w["ln_g"], w["ln_b"], layer, jl)
            new_h.append(s)
        else:
            extras = [
                (w["gla_w2"], _resident((None, GLA_RANK_PAD, GLA_QK), row)),
                (w["gla_bg"], _resident((None, 1, GLA_QK), row)),
                (w["gla_norm"], _resident((None, 1, GLA_V), row)),
            ]
            x, s = _mixer_call("gla", x, w["gla_in"], w["gla_out"], mem_k, mem_v, s_gla,
                               extras, w["ln_g"], w["ln_b"], layer, jl)
            new_g.append(s)
        xf = _ffn_call(x.reshape(b * l, d), w["ffn_g"], w["ffn_u"], w["ffn_d"], w["ln_g"], w["ln_b"], layer, 1, 2)
        x = xf.reshape(b, l, d)
    return x, jnp.stack(new_h), jnp.stack(new_g)


def _prepare_weights(ffn_w_gate, ffn_w_up, ffn_w_down, ln_gain, ln_bias, hgrn_w_in, hgrn_lb_logits, hgrn_norm,
                     hgrn_w_out, gla_w_in, gla_w_gate2, gla_b_gate, gla_norm, gla_w_out):
    n_gla = gla_w_in.shape[0]
    pad_cols = jnp.zeros((n_gla, D_MODEL, GLA_RANK_PAD - GLA_GATE_RANK), gla_w_in.dtype)
    ga_end = 2 * GLA_QK + 2 * GLA_V + GLA_GATE_RANK
    gla_in = jnp.concatenate([gla_w_in[:, :, :ga_end], pad_cols, gla_w_in[:, :, ga_end:]], axis=-1)
    pad_rows = jnp.zeros((n_gla, GLA_RANK_PAD - GLA_GATE_RANK, GLA_QK), gla_w_gate2.dtype)
    gla_w2 = jnp.concatenate([gla_w_gate2, pad_rows], axis=1)
    return {
        "ffn_g": ffn_w_gate.astype(BF16), "ffn_u": ffn_w_up.astype(BF16), "ffn_d": ffn_w_down.astype(BF16),
        "ln_g": ln_gain.reshape(DEPTH * 3, 1, D_MODEL), "ln_b": ln_bias.reshape(DEPTH * 3, 1, D_MODEL),
        "hgrn_in": hgrn_w_in.astype(BF16), "hgrn_lb": hgrn_lb_logits,
        "hgrn_norm": hgrn_norm.reshape(-1, 1, HGRN_WIDTH), "hgrn_out": hgrn_w_out.astype(BF16),
        "gla_in": gla_in.astype(BF16), "gla_w2": gla_w2.astype(BF16),
        "gla_bg": gla_b_gate.reshape(-1, 1, GLA_QK), "gla_norm": gla_norm.reshape(-1, 1, GLA_V),
        "gla_out": gla_w_out.astype(BF16),
    }


def kernel(x_prompt, x_sample, mem_prompt, cache_mem_k, cache_mem_v, state_hgrn, state_gla,
           ffn_w_gate, ffn_w_up, ffn_w_down, ln_gain, ln_bias,
           hgrn_w_in, hgrn_lb_logits, hgrn_norm, hgrn_w_out,
           gla_w_in, gla_w_gate2, gla_b_gate, gla_norm, gla_w_out, mem_w_k, mem_w_v):
    w = _prepare_weights(ffn_w_gate, ffn_w_up, ffn_w_down, ln_gain, ln_bias, hgrn_w_in, hgrn_lb_logits,
                         hgrn_norm, hgrn_w_out, gla_w_in, gla_w_gate2, gla_b_gate, gla_norm, gla_w_out)
    b = x_prompt.shape[0]
    db = x_sample.shape[0]
    mem_k_flat, mem_v_flat = _memkv_call(mem_prompt.reshape(b * MEM_TOKENS, D_MODEL),
                                         mem_w_k.astype(BF16), mem_w_v.astype(BF16))
    mem_k_p = mem_k_flat.reshape(DEPTH, b, MEM_TOKENS, MEM_WIDTH)
    mem_v_p = mem_v_flat.reshape(DEPTH, b, MEM_TOKENS, MEM_WIDTH)
    n_h, n_g = state_hgrn.shape[0], state_gla.shape[0]
    zeros_h = jnp.zeros((n_h, b) + state_hgrn.shape[2:], F32)
    zeros_g = jnp.zeros((n_g, b) + state_gla.shape[2:], F32)
    y_p, sh_p, sg_p = _run_trunk(x_prompt, mem_k_p, mem_v_p, zeros_h, zeros_g, w)
    y_s, sh_s, sg_s = _run_trunk(x_sample,
                                 cache_mem_k.reshape(DEPTH, db, MEM_TOKENS, MEM_WIDTH),
                                 cache_mem_v.reshape(DEPTH, db, MEM_TOKENS, MEM_WIDTH),
                                 state_hgrn, state_gla, w)
    kv_shape = (DEPTH, b, MEM_TOKENS, MEM_HEADS, MEM_HEAD_DIM)
    return (y_p, y_s, sh_p, sg_p, mem_k_flat.reshape(kv_shape), mem_v_flat.reshape(kv_shape), sh_s, sg_s)
```

```python
import functools

import jax
import jax.numpy as jnp
import numpy as np
from jax import lax
from jax.experimental import pallas as pl
from jax.experimental.pallas import tpu as pltpu

F32 = jnp.float32
BF16 = jnp.bfloat16

D_MODEL = 1024
DEPTH = 4
CHUNK = 64
N_MIXERS = 2
LN_PER_LAYER = 3
HGRN_HEAD_DIM = 128
HGRN_HEADS = D_MODEL // HGRN_HEAD_DIM
HGRN_WIDTH = HGRN_HEADS * HGRN_HEAD_DIM
GLA_HEADS = 4
GLA_KEY_DIM = D_MODEL // (2 * GLA_HEADS)
GLA_VAL_DIM = D_MODEL // GLA_HEADS
GLA_GATE_RANK = 16
GLA_TAU = 16.0
GLA_QK = GLA_HEADS * GLA_KEY_DIM
GLA_V = GLA_HEADS * GLA_VAL_DIM
MEM_TOKENS = 256
MEM_HEADS = 4
MEM_HEAD_DIM = 128
MEM_WIDTH = MEM_HEADS * MEM_HEAD_DIM
D_FF = 2816
ALPHA = (2.0 * DEPTH) ** 0.25
LN_EPS = 1e-5
RMS_EPS = 1e-6
GATE_CLAMP = 1.0 - 1e-6

LANES = 128
SUBLANES = 8
BF16_TILE_ROWS = 16
LOG2E = 1.4426950408889634
GLA_RANK_PAD = LANES
HGRN_Q, HGRN_F, HGRN_I, HGRN_G, HGRN_XQ = (i * HGRN_WIDTH for i in range(5))
HGRN_IN = 4 * HGRN_WIDTH + MEM_WIDTH
GLA_Q, GLA_K, GLA_VO = 0, GLA_QK, 2 * GLA_QK
GLA_R = GLA_VO + GLA_V
GLA_GA = GLA_R + GLA_V
GLA_XQ = GLA_GA + GLA_RANK_PAD
GLA_IN_PAD = GLA_XQ + MEM_WIDTH
MIX_WIDTH = D_MODEL + MEM_WIDTH

VMEM_LIMIT_BYTES = 56 * 1024 * 1024
MXU_TILE = 256
TOKEN_BLOCK = 512
FFN_LN_PIECES = 8
SEQ_BLOCK = 512
PROJ_PIECE = 512
_LEVEL_HALVES = (32, 16, 8, 4, 2, 1)


def _dot(a, b):
    return jnp.dot(a, b, preferred_element_type=F32)


def _dot_nt(a, b):
    return lax.dot_general(a, b, (((1,), (1,)), ((), ())), preferred_element_type=F32)


def _layer_norm(z, gain, bias):
    mu = jnp.mean(z, axis=-1, keepdims=True)
    zc = z - mu
    var = jnp.mean(zc * zc, axis=-1, keepdims=True)
    return zc * lax.rsqrt(var + LN_EPS) * gain + bias


def _silu(x):
    return x * jax.nn.sigmoid(x)


def _resident(block_shape, index_map):
    return pl.BlockSpec(block_shape, index_map, pipeline_mode=pl.Buffered(1))


def _ffn_kernel(x_ref, xr_ref, wg_ref, wu_ref, wd_ref, gain_ref, bias_ref, o_ref, y_scr, h_scr):
    i = pl.program_id(0)
    last = pl.num_programs(0) - 1
    rows = x_ref.shape[0]
    col_pieces = [pl.ds(c0, min(MXU_TILE, D_FF - c0)) for c0 in range(0, D_FF, MXU_TILE)]
    ln_rows = rows // FFN_LN_PIECES
    row_pieces = [pl.ds(r0, ln_rows) for r0 in range(0, rows, ln_rows)]

    def finish_previous(rs):
        z = ALPHA * xr_ref[rs, :] + 0.5 * y_scr[rs, :]
        o_ref[rs, :] = _layer_norm(z, gain_ref[...], bias_ref[...])

    def block(with_previous):
        xb = x_ref[...].astype(BF16)
        for p, cols in enumerate(col_pieces):
            g = _dot(xb, wg_ref[:, cols])
            u = _dot(xb, wu_ref[:, cols])
            h_scr[:, cols] = (_silu(g) * u).astype(BF16)
            if with_previous and p < len(row_pieces):
                finish_previous(row_pieces[p])
        y_scr[...] = _dot(h_scr[...], wd_ref[...])

    pl.when(i == 0)(lambda: block(False))
    pl.when((i > 0) & (i < last))(lambda: block(True))

    @pl.when(i == last)
    def _():
        for rs in row_pieces:
            finish_previous(rs)


def _ffn_call(x, wg, wu, wd, ln_g, ln_b, layer, which, ln_idx):
    t, d = x.shape
    tt = min(TOKEN_BLOCK, t)
    assert t % tt == 0
    n = t // tt
    widx = lambda i: (layer, which, 0, 0)
    lidx = lambda i: (layer * LN_PER_LAYER + ln_idx, 0, 0)
    prev = pl.BlockSpec((tt, d), lambda i: (jnp.maximum(i - 1, 0), 0))
    return pl.pallas_call(
        _ffn_kernel,
        grid=(n + 1,),
        in_specs=[
            pl.BlockSpec((tt, d), lambda i: (jnp.minimum(i, n - 1), 0)),
            prev,
            _resident((None, None, d, D_FF), widx),
            _resident((None, None, d, D_FF), widx),
            _resident((None, None, D_FF, d), widx),
            _resident((None, 1, d), lidx),
            _resident((None, 1, d), lidx),
        ],
        out_specs=prev,
        out_shape=jax.ShapeDtypeStruct((t, d), F32),
        scratch_shapes=[pltpu.VMEM((tt, d), F32), pltpu.VMEM((tt, D_FF), BF16)],
        compiler_params=pltpu.CompilerParams(
            dimension_semantics=("arbitrary",), vmem_limit_bytes=VMEM_LIMIT_BYTES),
        name="swiglu_sublayer",
    )(x, x, wg, wu, wd, ln_g, ln_b)


def _memkv_kernel(m_ref, wk_ref, wv_ref, k_ref, v_ref):
    mb = m_ref[...].astype(BF16)
    k_ref[...] = _dot(mb, wk_ref[...])
    v_ref[...] = _dot(mb, wv_ref[...])


def _memkv_call(mem, wk, wv):
    t, d = mem.shape
    wspec = pl.BlockSpec((None, d, MEM_WIDTH), lambda l: (l, 0, 0))
    ospec = pl.BlockSpec((None, t, MEM_WIDTH), lambda l: (l, 0, 0))
    oshape = jax.ShapeDtypeStruct((DEPTH, t, MEM_WIDTH), F32)
    return pl.pallas_call(
        _memkv_kernel,
        grid=(DEPTH,),
        in_specs=[_resident((t, d), lambda l: (0, 0)), wspec, wspec],
        out_specs=[ospec, ospec],
        out_shape=[oshape, oshape],
        compiler_params=pltpu.CompilerParams(
            dimension_semantics=("parallel",), vmem_limit_bytes=VMEM_LIMIT_BYTES),
        name="memory_kv_projection",
    )(mem, wk, wv)


def _level_mask(t, s, h):
    if h == 0:
        return t == s
    lg = h.bit_length() - 1
    ts, ss = t >> lg, s >> lg
    return (ts == ss + 1) & ((ss & 1) == 0)


def _chunk_consts(kd):
    t_np, s_np = np.indices((CHUNK, CHUNK))
    s = lax.broadcasted_iota(jnp.int32, (SUBLANES, CHUNK), 1)
    sub_t = lax.broadcasted_iota(jnp.int32, (SUBLANES, CHUNK), 0)
    shared = {}
    masks = {}
    for h in (0,) + _LEVEL_HALVES:
        full = np.asarray(_level_mask(t_np, s_np, h))
        blocks = []
        for r in range(CHUNK // SUBLANES):
            block = full[r * SUBLANES:(r + 1) * SUBLANES]
            if not block.any():
                blocks.append(None)
                continue
            key = block.tobytes()
            if key not in shared:
                shared[key] = _level_mask(sub_t + r * SUBLANES, s, h)
            blocks.append(shared[key])
        masks[h] = blocks
    t = lax.broadcasted_iota(jnp.int32, (CHUNK, CHUNK), 0)
    s_full = lax.broadcasted_iota(jnp.int32, (CHUNK, CHUNK), 1)
    sub = lax.broadcasted_iota(jnp.int32, (1, SUBLANES, kd), 1)
    return {"masks": masks, "tri": (t >= s_full).astype(BF16), "sub": sub}


def _cumsum_rows(tri, g):
    g1 = g.astype(BF16)
    g2 = (g - g1.astype(F32)).astype(BF16)
    return _dot(tri, g1) + _dot(tri, g2)


def _neg_abs(x):
    bits = lax.bitcast_convert_type(x, jnp.int32) | jnp.int32(-2 ** 31)
    return lax.bitcast_convert_type(bits, F32)


def _level_operand(q, k, b2, g2, h, sub):
    kd = q.shape[-1]
    parent = 2 * h
    if parent >= 2 * SUBLANES:
        n = CHUNK // parent
        b4 = b2.reshape(n, parent, kd)
        ref = b4[:, h - 1:h, :]
        x = jnp.concatenate([ref - b4[:, :h, :], b4[:, h:, :] - ref], axis=1)
        src = jnp.concatenate([k.reshape(n, parent, kd)[:, :h, :], q.reshape(n, parent, kd)[:, h:, :]], axis=1)
        return (src * jnp.exp2(x)).astype(BF16).reshape(CHUNK, kd)
    n = CHUNK // SUBLANES
    b3, q3, k3 = (a.reshape(n, SUBLANES, kd) for a in (b2, q, k))
    upper = (sub & h) != 0
    if h == 4:
        x = _neg_abs(b3 - b3[:, 3:4, :])
    elif h == 2:
        x = _neg_abs(b3 - jnp.where(sub < 4, b3[:, 1:2, :], b3[:, 5:6, :]))
    else:
        x = jnp.where(upper, g2.reshape(n, SUBLANES, kd), 0.0)
    return (jnp.where(upper, q3, k3) * jnp.exp2(x)).astype(BF16).reshape(CHUNK, kd)


def _intra_chunk_scores(q, k, b2, g2, consts):
    n = CHUNK // SUBLANES
    blocks = [None] * n
    for h in (0,) + _LEVEL_HALVES:
        if h == 0:
            p = _dot_nt(q.astype(BF16), k.astype(BF16))
        else:
            z = _level_operand(q, k, b2, g2, h, consts["sub"])
            p = _dot_nt(z, z)
        for r, mask in enumerate(consts["masks"][h]):
            if mask is not None:
                rows = p[r * SUBLANES:(r + 1) * SUBLANES, :]
                blocks[r] = jnp.where(mask, rows, 0.0 if blocks[r] is None else blocks[r])
    return jnp.concatenate(blocks, axis=0).astype(BF16)


def _stage_head(stage, h, q, k, g, tri):
    stage_f, stage_b, stage_d = stage
    kd = q.shape[-1]
    cs = pl.ds(h * kd, kd)
    b2 = _cumsum_rows(tri, g) * LOG2E
    b2_last = b2[CHUNK - 1:CHUNK, :]
    stage_f[0, :, cs] = q
    stage_f[1, :, cs] = k
    stage_f[2, :, cs] = b2
    stage_f[3, :, cs] = g * LOG2E
    stage_b[0, :, cs] = (q * jnp.exp2(b2)).astype(BF16)
    stage_b[1, :, cs] = (k * jnp.exp2(b2_last - b2)).astype(BF16)
    stage_d[:, cs] = jnp.exp2(b2_last)


def _recurrence_chunk(stage, load_v, st_scr, consts, heads, kd, with_scores, after_scores, emit):
    stage_f, stage_b, stage_d = stage
    scores = []
    for h in range(heads):
        cs = pl.ds(h * kd, kd)
        scores.append(_intra_chunk_scores(stage_f[0, :, cs], stage_f[1, :, cs], stage_f[2, :, cs],
                                          stage_f[3, :, cs], consts))
        with_scores(h)
    carried = [(stage_b[0, :, pl.ds(h * kd, kd)], stage_b[1, :, pl.ds(h * kd, kd)], stage_d[:, pl.ds(h * kd, kd)])
               for h in range(heads)]
    after_scores()
    for h in range(heads):
        q_in, k_out, decay = carried[h]
        v = load_v(h)
        st = st_scr[h]
        o = _dot(scores[h], v.astype(BF16)) + _dot_nt(q_in, st.astype(BF16))
        st_scr[h] = st * decay + _dot(v.T.astype(BF16), k_out)
        emit(h, o)


def _head_norm_gate(o, gain, gate):
    o = o * lax.rsqrt(jnp.mean(o * o, axis=-1, keepdims=True) + RMS_EPS)
    return o * gain * _silu(gate)


def _log_sigmoid(z):
    return -(jnp.maximum(-z, 0.0) + jnp.log1p(jnp.exp(-jnp.abs(z))))


def _hgrn_lower_bound(logits, j):
    m = jnp.max(logits, axis=0, keepdims=True)
    e = jnp.exp(logits - m)
    sm = e / jnp.sum(e, axis=0, keepdims=True)
    lb = jnp.zeros_like(sm[0:1])
    for i in range(1, j + 1):
        lb = lb + sm[i:i + 1]
    return lb


def _project_and_attend(x_ref, win_ref, proj_scr, mix_scr, mk_ref, mv_ref, main_width):
    xb = x_ref[...].astype(BF16)
    n_in = proj_scr.shape[-1]
    xq_off = n_in - MEM_WIDTH
    proj_scr[:, pl.ds(main_width, n_in - main_width)] = _dot(xb, win_ref[:, pl.ds(main_width, n_in - main_width)])
    starts = list(range(0, main_width, PROJ_PIECE))
    heads_after = {(i * len(starts)) // MEM_HEADS: i for i in range(MEM_HEADS)}
    assert len(heads_after) == MEM_HEADS
    head_cols = [pl.ds(h * MEM_HEAD_DIM, MEM_HEAD_DIM) for h in range(MEM_HEADS)]
    scores = [_dot_nt(proj_scr[:, pl.ds(xq_off + h * MEM_HEAD_DIM, MEM_HEAD_DIM)].astype(BF16),
                      mk_ref[:, head_cols[h]].astype(BF16)) for h in range(MEM_HEADS)]
    pending = None

    def read_out(h, p):
        xo = _dot(p, mv_ref[:, head_cols[h]].astype(BF16))
        mix_scr[:, pl.ds(D_MODEL + h * MEM_HEAD_DIM, MEM_HEAD_DIM)] = xo.astype(BF16)

    for n, c0 in enumerate(starts):
        cols = pl.ds(c0, min(PROJ_PIECE, main_width - c0))
        proj_scr[:, cols] = _dot(xb, win_ref[:, cols])
        if pending is not None:
            read_out(*pending)
            pending = None
        if n in heads_after:
            h = heads_after[n]
            s = scores[h] * MEM_HEAD_DIM ** -0.5
            p = jnp.exp(s - jnp.max(s, axis=-1, keepdims=True))
            pending = (h, (p / jnp.sum(p, axis=-1, keepdims=True)).astype(BF16))
    if pending is not None:
        read_out(*pending)


def _output(x_ref, mix_scr, wout_ref, lng_ref, lnb_ref, o_ref, finish_last_chunk):
    rows = x_ref.shape[0]
    half = rows // 2 if rows % (2 * BF16_TILE_ROWS) == 0 and rows // 2 >= CHUNK else rows
    pieces = [pl.ds(r0, half) for r0 in range(0, rows, half)]
    ys = []
    for n, rs in enumerate(pieces):
        if n == len(pieces) - 1:
            finish_last_chunk()
        ys.append(_dot(mix_scr[rs, :], wout_ref[...]))
    for rs, y in zip(pieces, ys):
        o_ref[rs, :] = _layer_norm(ALPHA * x_ref[rs, :] + y, lng_ref[...], lnb_ref[...])


def _chunk_loop(stage, restage, proj_scr, mix_scr, st_scr, o_scr, ng_ref, consts, n_chunks, *, heads, kd, vd,
                v_off, gate_off):
    first = pl.ds(0, CHUNK)
    for h in range(heads):
        restage(h, first)
    o_scr[...] = jnp.zeros(o_scr.shape, F32)

    def finish(h, rows):
        hv = pl.ds(h * vd, vd)
        gate = proj_scr[rows, pl.ds(gate_off + h * vd, vd)]
        mix_scr[rows, hv] = _head_norm_gate(o_scr[:, hv], ng_ref[:, h * vd:(h + 1) * vd], gate).astype(BF16)

    def store_output(h, o):
        o_scr[:, pl.ds(h * vd, vd)] = o

    def chunk(c, carry):
        rows = pl.ds(pl.multiple_of(c * CHUNK, CHUNK), CHUNK)
        prv = pl.ds(pl.multiple_of(jnp.maximum(c - 1, 0) * CHUNK, CHUNK), CHUNK)
        nxt = pl.ds(pl.multiple_of(jnp.minimum(c + 1, n_chunks - 1) * CHUNK, CHUNK), CHUNK)
        load_v = lambda h: proj_scr[rows, pl.ds(v_off + h * vd, vd)]

        def after_scores():
            for h in range(heads):
                restage(h, nxt)

        _recurrence_chunk(stage, load_v, st_scr, consts, heads, kd, lambda h: finish(h, prv), after_scores,
                          store_output)
        return carry

    lax.fori_loop(0, n_chunks, chunk, 0)

    def finish_last_chunk():
        for h in range(heads):
            finish(h, pl.ds((n_chunks - 1) * CHUNK, CHUNK))

    return finish_last_chunk


def _load_state(s0_ref, st_scr, heads):
    for h in range(heads):
        st_scr[h] = s0_ref[h].T


def _store_state(st_scr, sout_ref, heads):
    for h in range(heads):
        sout_ref[h] = st_scr[h].T


def _hgrn_kernel(x_ref, win_ref, wout_ref, mk_ref, mv_ref, s0_ref, lbl_ref, ng_ref, lng_ref, lnb_ref,
                 o_ref, sout_ref, proj_scr, mix_scr, st_scr, stage_f, stage_b, stage_d, o_scr, *, layer_j, n_chunks):
    j = pl.program_id(1)

    @pl.when(j == 0)
    def _():
        _load_state(s0_ref, st_scr, HGRN_HEADS)

    key_scale = 1.0 - _hgrn_lower_bound(lbl_ref[...], layer_j)
    consts = _chunk_consts(HGRN_HEAD_DIM)
    hd = HGRN_HEAD_DIM
    stage = (stage_f, stage_b, stage_d)

    def restage(h, rows):
        q = _silu(proj_scr[rows, pl.ds(HGRN_Q + h * hd, hd)]) * hd ** -0.5
        k = key_scale[:, h * hd:(h + 1) * hd] * jax.nn.sigmoid(-proj_scr[rows, pl.ds(HGRN_F + h * hd, hd)])
        _stage_head(stage, h, q, k, jnp.log(1.0 - jnp.minimum(k, GATE_CLAMP)), consts["tri"])

    _project_and_attend(x_ref, win_ref, proj_scr, mix_scr, mk_ref, mv_ref, HGRN_XQ)
    finish_last_chunk = _chunk_loop(stage, restage, proj_scr, mix_scr, st_scr, o_scr, ng_ref, consts, n_chunks,
                                    heads=HGRN_HEADS, kd=hd, vd=hd, v_off=HGRN_I, gate_off=HGRN_G)
    _output(x_ref, mix_scr, wout_ref, lng_ref, lnb_ref, o_ref, finish_last_chunk)

    @pl.when(j == pl.num_programs(1) - 1)
    def _():
        _store_state(st_scr, sout_ref, HGRN_HEADS)


def _gla_kernel(x_ref, win_ref, wout_ref, mk_ref, mv_ref, s0_ref, w2_ref, bg_ref, ng_ref, lng_ref, lnb_ref,
                o_ref, sout_ref, proj_scr, mix_scr, st_scr, stage_f, stage_b, stage_d, o_scr, *, n_chunks):
    j = pl.program_id(1)

    @pl.when(j == 0)
    def _():
        _load_state(s0_ref, st_scr, GLA_HEADS)

    consts = _chunk_consts(GLA_KEY_DIM)
    kd, vd = GLA_KEY_DIM, GLA_VAL_DIM
    stage = (stage_f, stage_b, stage_d)

    def restage(h, rows):
        cs = pl.ds(h * kd, kd)
        ga = proj_scr[rows, pl.ds(GLA_GA, GLA_RANK_PAD)].astype(BF16)
        g = _log_sigmoid(_dot(ga, w2_ref[:, cs]) + bg_ref[:, cs]) / GLA_TAU
        q = proj_scr[rows, pl.ds(GLA_Q + h * kd, kd)] * kd ** -0.5
        _stage_head(stage, h, q, proj_scr[rows, pl.ds(GLA_K + h * kd, kd)], g, consts["tri"])

    _project_and_attend(x_ref, win_ref, proj_scr, mix_scr, mk_ref, mv_ref, GLA_GA)

    finish_last_chunk = _chunk_loop(stage, restage, proj_scr, mix_scr, st_scr, o_scr, ng_ref, consts, n_chunks,
                                    heads=GLA_HEADS, kd=kd, vd=vd, v_off=GLA_VO, gate_off=GLA_R)
    _output(x_ref, mix_scr, wout_ref, lng_ref, lnb_ref, o_ref, finish_last_chunk)

    @pl.when(j == pl.num_programs(1) - 1)
    def _():
        _store_state(st_scr, sout_ref, GLA_HEADS)


def _mixer_call(kind, x, w_in, w_out, mem_k, mem_v, s0, extras, ln_g, ln_b, layer, layer_j):
    b, l, d = x.shape
    tt = min(SEQ_BLOCK, l)
    assert l % tt == 0 and tt % CHUNK == 0
    n_in = w_in.shape[-1]
    heads, kd, vd = s0.shape[2:]
    xspec = pl.BlockSpec((None, tt, d), lambda bi, j: (bi, j, 0))
    mspec = pl.BlockSpec((None, None, MEM_TOKENS, MEM_WIDTH), lambda bi, j: (layer, bi, 0, 0))
    s0spec = pl.BlockSpec((None, None, heads, kd, vd), lambda bi, j: (layer_j, bi, 0, 0, 0))
    sspec = pl.BlockSpec((None, heads, kd, vd), lambda bi, j: (bi, 0, 0, 0))
    lnspec = _resident((None, 1, d), lambda bi, j: (layer * LN_PER_LAYER + 1, 0, 0))
    if kind == "hgrn":
        body = functools.partial(_hgrn_kernel, layer_j=layer_j, n_chunks=tt // CHUNK)
    else:
        body = functools.partial(_gla_kernel, n_chunks=tt // CHUNK)
    extra_arrays = [a for a, _ in extras]
    extra_specs = [s for _, s in extras]
    return pl.pallas_call(
        body,
        grid=(b, l // tt),
        in_specs=[
            xspec,
            _resident((None, d, n_in), lambda bi, j: (layer_j, 0, 0)),
            _resident((None, MIX_WIDTH, d), lambda bi, j: (layer_j, 0, 0)),
            mspec, mspec, s0spec,
            *extra_specs,
            lnspec, lnspec,
        ],
        out_specs=[xspec, sspec],
        out_shape=[jax.ShapeDtypeStruct((b, l, d), F32), jax.ShapeDtypeStruct(s0.shape[1:], F32)],
        scratch_shapes=[
            pltpu.VMEM((tt, n_in), F32),
            pltpu.VMEM((tt, MIX_WIDTH), BF16),
            pltpu.VMEM((heads, vd, kd), F32),
            pltpu.VMEM((4, CHUNK, heads * kd), F32),
            pltpu.VMEM((2, CHUNK, heads * kd), BF16),
            pltpu.VMEM((1, heads * kd), F32),
            pltpu.VMEM((CHUNK, heads * vd), F32),
        ],
        compiler_params=pltpu.CompilerParams(
            dimension_semantics=("parallel", "arbitrary"), vmem_limit_bytes=VMEM_LIMIT_BYTES),
        name=kind + "_mixer_sublayer",
    )(x, w_in, w_out, mem_k, mem_v, s0, *extra_arrays, ln_g, ln_b)


def _run_trunk(x, mem_k, mem_v, s_hgrn, s_gla, w):
    b, l, d = x.shape
    new_h, new_g = [], []
    for layer in range(DEPTH):
        xf = _ffn_call(x.reshape(b * l, d), w["ffn_g"], w["ffn_u"], w["ffn_d"], w["ln_g"], w["ln_b"], layer, 0, 0)
        x = xf.reshape(b, l, d)
        jl = layer // N_MIXERS
        row = lambda bi, j: (jl, 0, 0)
        if layer % N_MIXERS == 0:
            extras = [
                (w["hgrn_lb"], _resident(w["hgrn_lb"].shape, lambda bi, j: (0, 0))),
                (w["hgrn_norm"], _resident((None, 1, HGRN_WIDTH), row)),
            ]
            x, s = _mixer_call("hgrn", x, w["hgrn_in"], w["hgrn_out"], mem_k, mem_v, s_hgrn,
                               extras, w["ln_g"], w["ln_b"], layer, jl)
            new_h.append(s)
        else:
            extras = [
                (w["gla_w2"], _resident((None, GLA_RANK_PAD, GLA_QK), row)),
                (w["gla_bg"], _resident((None, 1, GLA_QK), row)),
                (w["gla_norm"], _resident((None, 1, GLA_V), row)),
            ]
            x, s = _mixer_call("gla", x, w["gla_in"], w["gla_out"], mem_k, mem_v, s_gla,
                               extras, w["ln_g"], w["ln_b"], layer, jl)
            new_g.append(s)
        xf = _ffn_call(x.reshape(b * l, d), w["ffn_g"], w["ffn_u"], w["ffn_d"], w["ln_g"], w["ln_b"], layer, 1, 2)
        x = xf.reshape(b, l, d)
    return x, jnp.stack(new_h), jnp.stack(new_g)


def _prepare_weights(ffn_w_gate, ffn_w_up, ffn_w_down, ln_gain, ln_bias, hgrn_w_in, hgrn_lb_logits, hgrn_norm,
                     hgrn_w_out, gla_w_in, gla_w_gate2, gla_b_gate, gla_norm, gla_w_out):
    n_gla = gla_w_in.shape[0]
    pad_cols = jnp.zeros((n_gla, D_MODEL, GLA_RANK_PAD - GLA_GATE_RANK), gla_w_in.dtype)
    ga_end = 2 * GLA_QK + 2 * GLA_V + GLA_GATE_RANK
    gla_in = jnp.concatenate([gla_w_in[:, :, :ga_end], pad_cols, gla_w_in[:, :, ga_end:]], axis=-1)
    pad_rows = jnp.zeros((n_gla, GLA_RANK_PAD - GLA_GATE_RANK, GLA_QK), gla_w_gate2.dtype)
    gla_w2 = jnp.concatenate([gla_w_gate2, pad_rows], axis=1)
    return {
        "ffn_g": ffn_w_gate.astype(BF16), "ffn_u": ffn_w_up.astype(BF16), "ffn_d": ffn_w_down.astype(BF16),
        "ln_g": ln_gain.reshape(DEPTH * LN_PER_LAYER, 1, D_MODEL),
        "ln_b": ln_bias.reshape(DEPTH * LN_PER_LAYER, 1, D_MODEL),
        "hgrn_in": hgrn_w_in.astype(BF16), "hgrn_lb": hgrn_lb_logits,
        "hgrn_norm": hgrn_norm.reshape(-1, 1, HGRN_WIDTH), "hgrn_out": hgrn_w_out.astype(BF16),
        "gla_in": gla_in.astype(BF16), "gla_w2": gla_w2.astype(BF16),
        "gla_bg": gla_b_gate.reshape(-1, 1, GLA_QK), "gla_norm": gla_norm.reshape(-1, 1, GLA_V),
        "gla_out": gla_w_out.astype(BF16),
    }


def kernel(x_prompt, x_sample, mem_prompt, cache_mem_k, cache_mem_v, state_hgrn, state_gla,
           ffn_w_gate, ffn_w_up, ffn_w_down, ln_gain, ln_bias,
           hgrn_w_in, hgrn_lb_logits, hgrn_norm, hgrn_w_out,
           gla_w_in, gla_w_gate2, gla_b_gate, gla_norm, gla_w_out, mem_w_k, mem_w_v):
    w = _prepare_weights(ffn_w_gate, ffn_w_up, ffn_w_down, ln_gain, ln_bias, hgrn_w_in, hgrn_lb_logits,
                         hgrn_norm, hgrn_w_out, gla_w_in, gla_w_gate2, gla_b_gate, gla_norm, gla_w_out)
    b = x_prompt.shape[0]
    db = x_sample.shape[0]
    mem_k_flat, mem_v_flat = _memkv_call(mem_prompt.reshape(b * MEM_TOKENS, D_MODEL),
                                         mem_w_k.astype(BF16), mem_w_v.astype(BF16))
    mem_k_p = mem_k_flat.reshape(DEPTH, b, MEM_TOKENS, MEM_WIDTH)
    mem_v_p = mem_v_flat.reshape(DEPTH, b, MEM_TOKENS, MEM_WIDTH)
    n_h, n_g = state_hgrn.shape[0], state_gla.shape[0]
    zeros_h = jnp.zeros((n_h, b) + state_hgrn.shape[2:], F32)
    zeros_g = jnp.zeros((n_g, b) + state_gla.shape[2:], F32)
    y_p, sh_p, sg_p = _run_trunk(x_prompt, mem_k_p, mem_v_p, zeros_h, zeros_g, w)
    y_s, sh_s, sg_s = _run_trunk(x_sample,
                                 cache_mem_k.reshape(DEPTH, db, MEM_TOKENS, MEM_WIDTH),
                                 cache_mem_v.reshape(DEPTH, db, MEM_TOKENS, MEM_WIDTH),
                                 state_hgrn, state_gla, w)
    kv_shape = (DEPTH, b, MEM_TOKENS, MEM_HEADS, MEM_HEAD_DIM)
    return (y_p, y_s, sh_p, sg_p, mem_k_flat.reshape(kv_shape), mem_v_flat.reshape(kv_shape), sh_s, sg_s)
```
